```python
import math
import jax
import jax.numpy as jnp
from jax import lax
import numpy as np

D_MODEL = 1024
BATCH = 1
SEQ = 16384
DEPTH = 1
DEC_BATCH = 128
DEC_SEQ = 8
PAST_LEN = 16384
PAGE_SIZE = 128

HEAD_DIM = 64
A_SPAN = 128
A_DILATIONS = (1, 4, 16)
A_HEADS = 4
N_A_GROUPS = len(A_DILATIONS)
B_WINDOW = 128
B_Q_HEADS = 8
B_KV_HEADS = 2
N_BUCKETS = 32
MAX_DISTANCE = 2048
N_BIAS_HEADS = N_A_GROUPS * A_HEADS + B_Q_HEADS
BLK = 128
A_COLS = N_A_GROUPS * 3 * A_HEADS * HEAD_DIM
B_Q_COLS = B_Q_HEADS * HEAD_DIM
B_KV_COLS = B_KV_HEADS * HEAD_DIM
N_BRANCHES = 2
IN_COLS = A_COLS + B_Q_COLS + 2 * B_KV_COLS + N_BRANCHES * D_MODEL
N_SUBKEYS = 128
N_EXPERTS = N_SUBKEYS * N_SUBKEYS
PEER_HEADS = 8
PEER_TOPK = 16
PEER_QDIM = 256
PEER_HALF = PEER_QDIM // 2
PEER_BLOCK = 256
EPS = 1e-6
NEG = -1e30
SCALE = HEAD_DIM ** -0.5

kernel_name = 'hybrid_dilated_swa_peer_step'


def rms_norm(x, g):
    x32 = x.astype(jnp.float32)
    y = x32 * lax.rsqrt(jnp.mean(x32 * x32, axis=-1, keepdims=True) + EPS)
    return (y * g.astype(jnp.float32)).astype(x.dtype)


def t5_bucket(dist):
    exact = N_BUCKETS // 2
    d32 = jnp.maximum(dist, 1).astype(jnp.float32)
    large = exact + (jnp.log(d32 / exact) / math.log(MAX_DISTANCE / exact) * (N_BUCKETS - exact)).astype(jnp.int32)
    large = jnp.minimum(large, N_BUCKETS - 1)
    return jnp.where(dist < exact, dist, large)


def rel_bias(table_h, dist):
    return table_h[t5_bucket(dist)].astype(jnp.float32)


def band_bias(table_h, d, span):
    qo = jnp.arange(BLK)[:, None]
    ko = jnp.arange(2 * BLK)[None, :]
    sub = qo - ko + BLK
    ok = (sub >= 0) & (sub <= span)
    b = jnp.transpose(rel_bias(table_h, jnp.maximum(sub, 0) * d), (2, 0, 1))
    return jnp.where(ok[None], b, NEG)


def softmax_with_sink(s, sink):
    m = jnp.max(s, axis=-1)
    if sink is not None:
        m = jnp.maximum(m, sink)
    e = jnp.exp(s - m[..., None])
    den = jnp.sum(e, axis=-1)
    if sink is not None:
        den = den + jnp.exp(sink - m)
    return e / den[..., None], m + jnp.log(den)


def pad_len(t, n):
    return jnp.pad(t, [(0, 0), (0, n - t.shape[1])] + [(0, 0)] * (t.ndim - 2))


def band_attend(q, k, v, bias, sink=None):
    N, n0, H, hd = q.shape
    G = k.shape[2]
    R = H // G
    n = n0 + (-n0 % BLK)
    nb = n // BLK
    q, k, v = pad_len(q, n), pad_len(k, n), pad_len(v, n)

    def with_prev(t):
        prev = jnp.concatenate([jnp.zeros_like(t[:, :BLK]), t[:, :n - BLK]], axis=1).reshape(N, nb, BLK, G, hd)
        return jnp.concatenate([prev, t.reshape(N, nb, BLK, G, hd)], axis=2)

    kb, vb = with_prev(k), with_prev(v)
    s = jnp.einsum('nbqgrd,nbkgd->nbgrqk', q.reshape(N, nb, BLK, G, R, hd), kb,
                   preferred_element_type=jnp.float32) * SCALE
    key_pos = jnp.arange(nb)[:, None] * BLK + jnp.arange(2 * BLK)[None, :] - BLK
    s = s + bias.reshape(G, R, BLK, 2 * BLK) + jnp.where(key_pos >= 0, 0.0, NEG)[:, None, None, None, :]
    sink_b = None if sink is None else sink.astype(jnp.float32).reshape(G, R, 1)
    p, lse = softmax_with_sink(s, sink_b)
    out = jnp.einsum('nbgrqk,nbkgd->nbqgrd', p.astype(v.dtype), vb).reshape(N, n, H, hd)[:, :n0]
    lse = jnp.transpose(lse, (0, 1, 4, 2, 3)).reshape(N, n, H)[:, :n0]
    return out, lse


def dilated_prompt(q, k, v, d, bias, sink=None):
    N, S, H, hd = q.shape
    Sp = S + (-S % d)
    n = Sp // d

    def to_sub(t):
        c = t.shape[2]
        t = pad_len(t, Sp).reshape(N, n, d, c, hd)
        return jnp.transpose(t, (0, 2, 1, 3, 4)).reshape(N * d, n, c, hd)

    out, lse = band_attend(to_sub(q), to_sub(k), to_sub(v), bias, sink)
    out = jnp.transpose(out.reshape(N, d, n, H, hd), (0, 2, 1, 3, 4)).reshape(N, Sp, H, hd)[:, :S]
    lse = jnp.transpose(lse.reshape(N, d, n, H), (0, 2, 1, 3)).reshape(N, Sp, H)[:, :S]
    return out, lse


def window_sample(q, k_new, v_new, kv_buf, d, span, table_h, sink=None):
    Bd, T, H, hd = q.shape
    G = k_new.shape[2]
    R = H // G
    L = kv_buf.shape[1]
    kv_all = jnp.concatenate([kv_buf, jnp.stack([k_new, v_new], axis=2).astype(kv_buf.dtype)], axis=1)
    J = span // d + 1
    idx = L + jnp.arange(T)[:, None] - d * jnp.arange(J)[None, :]
    valid = idx >= 0
    kvg = kv_all[:, jnp.maximum(idx, 0)]
    s = jnp.einsum('btgrd,btjgd->btgrj', q.reshape(Bd, T, G, R, hd), kvg[:, :, :, 0],
                   preferred_element_type=jnp.float32) * SCALE
    bias = rel_bias(table_h, d * jnp.arange(J)).T.reshape(G, R, J)
    s = s + bias + jnp.where(valid, 0.0, NEG)[None, :, None, None, :]
    sink_b = None if sink is None else sink.astype(jnp.float32).reshape(G, R)
    p, lse = softmax_with_sink(s, sink_b)
    out = jnp.einsum('btgrj,btjgd->btgrd', p.astype(q.dtype), kvg[:, :, :, 1]).reshape(Bd, T, H, hd)
    return out, lse.reshape(Bd, T, H), kv_all[:, T:]


def project(xn, w_in, q_norm_a, k_norm_a, q_norm_b, k_norm_b):
    N, T, _ = xn.shape
    h = xn @ w_in
    a = h[..., :A_COLS].reshape(N, T, N_A_GROUPS, 3, A_HEADS, HEAD_DIM)
    qa = rms_norm(a[:, :, :, 0], q_norm_a[:, None, :])
    ka = rms_norm(a[:, :, :, 1], k_norm_a[:, None, :])
    va = a[:, :, :, 2]
    o = A_COLS
    qb = rms_norm(h[..., o:o + B_Q_COLS].reshape(N, T, B_Q_HEADS, HEAD_DIM), q_norm_b)
    o += B_Q_COLS
    kb = rms_norm(h[..., o:o + B_KV_COLS].reshape(N, T, B_KV_HEADS, HEAD_DIM), k_norm_b)
    o += B_KV_COLS
    vb = h[..., o:o + B_KV_COLS].reshape(N, T, B_KV_HEADS, HEAD_DIM)
    o += B_KV_COLS
    gates = jax.nn.sigmoid(h[..., o:].astype(jnp.float32)).astype(xn.dtype).reshape(N, T, N_BRANCHES, D_MODEL)
    return qa, ka, va, qb, kb, vb, gates


def peer_ffn(xn, w_peer_q, sub_keys_1, sub_keys_2, expert_u, expert_v):
    shp = xn.shape
    xf = xn.reshape(-1, D_MODEL)
    T = xf.shape[0]
    xf = jnp.pad(xf, ((0, -T % PEER_BLOCK), (0, 0)))
    xb = xf.reshape(-1, PEER_BLOCK, D_MODEL)

    def block(xt):
        q = (xt @ w_peer_q).reshape(PEER_BLOCK, PEER_HEADS, 2, PEER_HALF)
        s1 = jnp.einsum('thd,kd->thk', q[:, :, 0], sub_keys_1, preferred_element_type=jnp.float32)
        s2 = jnp.einsum('thd,kd->thk', q[:, :, 1], sub_keys_2, preferred_element_type=jnp.float32)
        v1, i1 = lax.top_k(s1, PEER_TOPK)
        v2, i2 = lax.top_k(s2, PEER_TOPK)
        cand = (v1[..., :, None] + v2[..., None, :]).reshape(PEER_BLOCK, PEER_HEADS, PEER_TOPK * PEER_TOPK)
        sc, ci = lax.top_k(cand, PEER_TOPK)
        e = (jnp.take_along_axis(i1, ci // PEER_TOPK, axis=-1) * N_SUBKEYS
             + jnp.take_along_axis(i2, ci % PEER_TOPK, axis=-1))
        g = jax.nn.softmax(sc, axis=-1)
        hu = jnp.einsum('thkd,td->thk', expert_u[e], xt, preferred_element_type=jnp.float32)
        act = (g * jax.nn.gelu(hu, approximate=False)).astype(xt.dtype)
        return jnp.einsum('thk,thkd->td', act, expert_v[e])

    out = lax.map(block, xb).reshape(-1, D_MODEL)[:T]
    return out.reshape(shp)


def finish(x, outs_a, lses_a, out_b, gates, w_up_a, w_up_b, w_out, norm2,
           w_peer_q, sub_keys_1, sub_keys_2, expert_u, expert_v):
    N, T, _ = x.shape
    wts = jax.nn.softmax(jnp.stack(lses_a, axis=0), axis=0)
    oa = jnp.sum(wts[..., None] * jnp.stack(outs_a, axis=0).astype(jnp.float32), axis=0).astype(x.dtype)
    ya = oa.reshape(N, T, A_HEADS * HEAD_DIM) @ w_up_a
    yb = out_b.reshape(N, T, B_Q_HEADS * HEAD_DIM) @ w_up_b
    merged = gates[:, :, 0] * ya + gates[:, :, 1] * yb
    h = x + merged @ w_out
    return h + peer_ffn(rms_norm(h, norm2), w_peer_q, sub_keys_1, sub_keys_2, expert_u, expert_v)


def setup_inputs(seed: int = 0) -> dict:
    key = jax.random.key(seed)
    ks = jax.random.split(key, 24)
    f32 = jnp.float32
    nrm = lambda k, shape, s: jax.random.normal(k, shape, f32) * s
    return {
        'x_prompt': nrm(ks[0], (BATCH, SEQ, D_MODEL), 1.0),
        'x_sample': nrm(ks[1], (DEC_BATCH, DEC_SEQ, D_MODEL), 1.0),
        'cache_a1_kv': nrm(ks[2], (DEC_BATCH, min(A_SPAN * A_DILATIONS[0], PAST_LEN), 2, A_HEADS, HEAD_DIM), 1.0),
        'cache_a2_kv': nrm(ks[3], (DEC_BATCH, min(A_SPAN * A_DILATIONS[1], PAST_LEN), 2, A_HEADS, HEAD_DIM), 1.0),
        'cache_a3_kv': nrm(ks[4], (DEC_BATCH, min(A_SPAN * A_DILATIONS[2], PAST_LEN), 2, A_HEADS, HEAD_DIM), 1.0),
        'cache_b_kv': nrm(ks[5], (DEC_BATCH, min(B_WINDOW, PAST_LEN), 2, B_KV_HEADS, HEAD_DIM), 1.0),
        'norm1': 1.0 + nrm(ks[6], (D_MODEL,), 0.1),
        'w_in': nrm(ks[7], (D_MODEL, IN_COLS), D_MODEL ** -0.5),
        'q_norm_a': 1.0 + nrm(ks[8], (N_A_GROUPS, HEAD_DIM), 0.1),
        'k_norm_a': 1.0 + nrm(ks[9], (N_A_GROUPS, HEAD_DIM), 0.1),
        'q_norm_b': 1.0 + nrm(ks[10], (HEAD_DIM,), 0.1),
        'k_norm_b': 1.0 + nrm(ks[11], (HEAD_DIM,), 0.1),
        'rel_bias_table': nrm(ks[12], (N_BUCKETS, N_BIAS_HEADS), 0.2),
        'sinks_b': nrm(ks[13], (B_Q_HEADS,), 0.5),
        'w_up_a': nrm(ks[14], (A_HEADS * HEAD_DIM, D_MODEL), (A_HEADS * HEAD_DIM) ** -0.5),
        'w_up_b': nrm(ks[15], (B_Q_HEADS * HEAD_DIM, D_MODEL), (B_Q_HEADS * HEAD_DIM) ** -0.5),
        'w_out': nrm(ks[16], (D_MODEL, D_MODEL), D_MODEL ** -0.5),
        'norm2': 1.0 + nrm(ks[17], (D_MODEL,), 0.1),
        'w_peer_q': nrm(ks[18], (D_MODEL, PEER_HEADS * PEER_QDIM), D_MODEL ** -0.5),
        'sub_keys_1': nrm(ks[19], (N_SUBKEYS, PEER_HALF), PEER_HALF ** -0.5),
        'sub_keys_2': nrm(ks[20], (N_SUBKEYS, PEER_HALF), PEER_HALF ** -0.5),
        'expert_u': nrm(ks[21], (N_EXPERTS, D_MODEL), D_MODEL ** -0.5),
        'expert_v': nrm(ks[22], (N_EXPERTS, D_MODEL), 0.5),
    }


def reference(x_prompt, x_sample, cache_a1_kv, cache_a2_kv, cache_a3_kv, cache_b_kv, norm1, w_in,
              q_norm_a, k_norm_a, q_norm_b, k_norm_b, rel_bias_table, sinks_b, w_up_a, w_up_b, w_out,
              norm2, w_peer_q, sub_keys_1, sub_keys_2, expert_u, expert_v):
    b_heads = slice(N_A_GROUPS * A_HEADS, N_BIAS_HEADS)
    for _layer in range(DEPTH):
        S = x_prompt.shape[1]
        qa, ka, va, qb, kb, vb, gates = project(rms_norm(x_prompt, norm1), w_in, q_norm_a, k_norm_a, q_norm_b, k_norm_b)
        outs_p, lses_p, new_p = [], [], []
        for g, d in enumerate(A_DILATIONS):
            tab = rel_bias_table[:, g * A_HEADS:(g + 1) * A_HEADS]
            o, l = dilated_prompt(qa[:, :, g], ka[:, :, g], va[:, :, g], d, band_bias(tab, d, A_SPAN))
            outs_p.append(o)
            lses_p.append(l)
            lp = min(A_SPAN * d, S)
            new_p.append(jnp.stack([ka[:, :, g], va[:, :, g]], axis=2)[:, S - lp:])
        out_b, _ = dilated_prompt(qb, kb, vb, 1, band_bias(rel_bias_table[:, b_heads], 1, B_WINDOW), sinks_b)
        lpb = min(B_WINDOW, S)
        new_b_prompt = jnp.stack([kb, vb], axis=2)[:, S - lpb:]
        y_prompt = finish(x_prompt, outs_p, lses_p, out_b, gates, w_up_a, w_up_b, w_out, norm2,
                          w_peer_q, sub_keys_1, sub_keys_2, expert_u, expert_v)

        qa, ka, va, qb, kb, vb, gates = project(rms_norm(x_sample, norm1), w_in, q_norm_a, k_norm_a, q_norm_b, k_norm_b)
        bufs = (cache_a1_kv, cache_a2_kv, cache_a3_kv)
        outs_s, lses_s, new_s = [], [], []
        for g, d in enumerate(A_DILATIONS):
            tab = rel_bias_table[:, g * A_HEADS:(g + 1) * A_HEADS]
            o, l, nb_ = window_sample(qa[:, :, g], ka[:, :, g], va[:, :, g], bufs[g], d, A_SPAN * d, tab)
            outs_s.append(o)
            lses_s.append(l)
            new_s.append(nb_)
        out_bs, _, new_b_sample = window_sample(qb, kb, vb, cache_b_kv, 1, B_WINDOW, rel_bias_table[:, b_heads], sinks_b)
        y_sample = finish(x_sample, outs_s, lses_s, out_bs, gates, w_up_a, w_up_b, w_out, norm2,
                          w_peer_q, sub_keys_1, sub_keys_2, expert_u, expert_v)
    return (y_prompt, y_sample, new_p[0], new_p[1], new_p[2], new_b_prompt,
            new_s[0], new_s[1], new_s[2], new_b_sample)
```

```python
import functools
import math

import jax
import jax.numpy as jnp
from jax import lax
from jax.experimental import pallas as pl
from jax.experimental.pallas import tpu as pltpu

f32 = jnp.float32
bf16 = jnp.bfloat16

D_MODEL = 1024
HEAD_DIM = 64
A_SPAN = 128
A_DILATIONS = (1, 4, 16)
A_HEADS = 4
N_A_GROUPS = 3
B_WINDOW = 128
B_Q_HEADS = 8
B_KV_HEADS = 2
N_BUCKETS = 32
MAX_DISTANCE = 2048
BLK = 128
DEC_SEQ = 8
A_GROUP_COLS = 3 * A_HEADS * HEAD_DIM
A_COLS = N_A_GROUPS * A_GROUP_COLS
B_Q_COLS = B_Q_HEADS * HEAD_DIM
B_KV_COLS = B_KV_HEADS * HEAD_DIM
QKV_COLS = A_COLS + B_Q_COLS + 2 * B_KV_COLS
GATE_COLS = 2 * D_MODEL
N_SUBKEYS = 128
N_EXPERTS = N_SUBKEYS * N_SUBKEYS
PEER_HEADS = 8
PEER_TOPK = 16
PEER_HALF = 128
EPS = 1e-6
NEG = -1e30
SCALE = HEAD_DIM ** -0.5

LANES = 128
SUBLANES = 8
MXU_COLS = 256
VMEM_LIMIT = 56 * 1024 * 1024

PROJ_TM = 256
PEER_T = 512
PEER_TC = 256
PEER_EB = 1024


def _const_spec(shape):
    nd = len(shape)
    return pl.BlockSpec(shape, lambda *_: (0,) * nd, pipeline_mode=pl.Buffered(1))


def _params(sem):
    return pltpu.CompilerParams(dimension_semantics=sem, vmem_limit_bytes=VMEM_LIMIT)


_NORM_ALL, _NORM_NONE, _NORM_FIRST_HALF = 0, 1, 2


def _chunk_norm_mode(c):
    col = c * MXU_COLS
    if col < A_COLS:
        return _NORM_NONE if (col % A_GROUP_COLS) == 2 * A_HEADS * HEAD_DIM else _NORM_ALL
    if col < A_COLS + B_Q_COLS:
        return _NORM_ALL
    return _NORM_FIRST_HALF


def _proj_kernel(x_ref, g1_ref, w_ref, gain_ref, seg_ref, qkv_ref, gate_ref):
    x = x_ref[...]
    ms = jnp.mean(x * x, axis=-1, keepdims=True)
    xn = ((x * lax.rsqrt(ms + EPS)) * g1_ref[...]).astype(bf16)
    seg = seg_ref[...]
    for c in range(QKV_COLS // MXU_COLS):
        cols = slice(c * MXU_COLS, (c + 1) * MXU_COLS)
        hb = jnp.dot(xn, w_ref[:, cols], preferred_element_type=f32)
        mode = _chunk_norm_mode(c)
        if mode != _NORM_NONE:
            h2 = hb * hb
            hi = h2.astype(bf16)
            lo = (h2 - hi.astype(f32)).astype(bf16)
            ss = (jnp.dot(hi, seg, preferred_element_type=f32)
                  + jnp.dot(lo, seg, preferred_element_type=f32))
            normed = (hb * lax.rsqrt(ss * (1.0 / HEAD_DIM) + EPS)) * gain_ref[:, cols]
            if mode == _NORM_ALL:
                hb = normed
            else:
                hb = jnp.concatenate([normed[:, :LANES], hb[:, LANES:]], axis=1)
        qkv_ref[:, cols] = hb
    hg = jnp.dot(xn, w_ref[:, QKV_COLS:], preferred_element_type=f32)
    gate_ref[...] = jax.nn.sigmoid(hg)


def _project(x, norm1, w_in_bf, gain, seg):
    n = x.shape[0]
    tm = PROJ_TM
    assert n % tm == 0
    return pl.pallas_call(
        _proj_kernel,
        out_shape=(jax.ShapeDtypeStruct((n, QKV_COLS), f32), jax.ShapeDtypeStruct((n, GATE_COLS), f32)),
        grid=(n // tm,),
        in_specs=[
            pl.BlockSpec((tm, D_MODEL), lambda i: (i, 0)),
            _const_spec((1, D_MODEL)),
            _const_spec((D_MODEL, QKV_COLS + GATE_COLS)),
            _const_spec((1, QKV_COLS)),
            _const_spec((MXU_COLS, MXU_COLS)),
        ],
        out_specs=(pl.BlockSpec((tm, QKV_COLS), lambda i: (i, 0)),
                   pl.BlockSpec((tm, GATE_COLS), lambda i: (i, 0))),
        compiler_params=_params(("parallel",)),
        name="project",
    )(x, norm1.reshape(1, D_MODEL), w_in_bf, gain, seg)


def _t5_bucket(dist):
    exact = N_BUCKETS // 2
    d32 = jnp.maximum(dist, 1).astype(f32)
    large = exact + (jnp.log(d32 / exact) / math.log(MAX_DISTANCE / exact) * (N_BUCKETS - exact)).astype(jnp.int32)
    large = jnp.minimum(large, N_BUCKETS - 1)
    return jnp.where(dist < exact, dist, large)


def _band_bias(table_h, d, span):
    qo = jnp.arange(BLK)[:, None]
    ko = jnp.arange(2 * BLK)[None, :]
    sub = qo - ko + BLK
    ok = (sub >= 0) & (sub <= span)
    b = jnp.transpose(table_h[_t5_bucket(jnp.maximum(sub, 0) * d)].astype(f32), (2, 0, 1))
    return jnp.where(ok[None], b, NEG)


def _sample_bias(table_h, d, steps, buf_len):
    t = jnp.arange(DEC_SEQ)[:, None]
    r = jnp.arange(buf_len + DEC_SEQ)[None, :]
    dist = buf_len + t - r
    ok = (dist >= 0) & (dist % d == 0) & (dist <= steps * d)
    b = jnp.transpose(table_h[_t5_bucket(jnp.maximum(dist, 0))].astype(f32), (2, 0, 1))
    b = jnp.where(ok[None], b, NEG).reshape(-1, buf_len + DEC_SEQ)
    old = b[:, :buf_len]
    new = jnp.pad(b[:, buf_len:], ((0, 0), (0, LANES - DEC_SEQ)), constant_values=NEG)
    return old, new


def _band_kernel(*refs, n_q, head_cfg, has_sink, want_lse):
    q_refs = refs[:n_q]
    kp_ref, kc_ref, vp_ref, vc_ref, bias_ref = refs[n_q:n_q + 5]
    pos = n_q + 5
    sink_ref = None
    if has_sink:
        sink_ref = refs[pos]
        pos += 1
    o_ref = refs[pos]
    l_ref = refs[pos + 1] if want_lse else None

    blk = pl.program_id(1)
    col = lax.broadcasted_iota(jnp.int32, (BLK, 2 * BLK), 1)
    no_prev = jnp.where((col < BLK) & (blk == 0), NEG, 0.0).astype(f32)
    contract_last = (((1,), (1,)), ((), ()))
    for h, (qi, qoff, kvoff) in enumerate(head_cfg):
        q = q_refs[qi][:, qoff:qoff + HEAD_DIM].astype(bf16)
        k = jnp.concatenate([kp_ref[:, kvoff:kvoff + HEAD_DIM], kc_ref[:, kvoff:kvoff + HEAD_DIM]], axis=0).astype(bf16)
        v = jnp.concatenate([vp_ref[:, kvoff:kvoff + HEAD_DIM], vc_ref[:, kvoff:kvoff + HEAD_DIM]], axis=0).astype(bf16)
        s = lax.dot_general(q, k, contract_last, preferred_element_type=f32) * SCALE + bias_ref[h] + no_prev
        m = jnp.max(s, axis=-1, keepdims=True)
        if has_sink:
            sink = sink_ref[h]
            m = jnp.maximum(m, sink)
        e = jnp.exp(s - m)
        den = jnp.sum(e, axis=-1, keepdims=True)
        if has_sink:
            den = den + jnp.exp(sink - m)
        p = (e / den).astype(bf16)
        o_ref[:, h * HEAD_DIM:(h + 1) * HEAD_DIM] = jnp.dot(p, v, preferred_element_type=f32)
        if want_lse:
            l_ref[:, h * HEAD_DIM:(h + 1) * HEAD_DIM] = jnp.broadcast_to(m + jnp.log(den), (BLK, HEAD_DIM))


def _band_attend_a(qkv, g, d, bias):
    s_len = qkv.shape[0]
    assert s_len % (d * BLK) == 0
    n = s_len // d
    width = A_HEADS * HEAD_DIM
    per_tok = QKV_COLS // width
    view = qkv.reshape(n, d * QKV_COLS)
    base = g * 3

    def spec(off, prev):
        if prev:
            return pl.BlockSpec((BLK, width), lambda r, b: (jnp.maximum(b - 1, 0), r * per_tok + base + off))
        return pl.BlockSpec((BLK, width), lambda r, b: (b, r * per_tok + base + off))

    head_cfg = tuple((0, h * HEAD_DIM, h * HEAD_DIM) for h in range(A_HEADS))
    out, lse = pl.pallas_call(
        functools.partial(_band_kernel, n_q=1, head_cfg=head_cfg, has_sink=False, want_lse=True),
        out_shape=(jax.ShapeDtypeStruct((n, d * width), f32), jax.ShapeDtypeStruct((n, d * width), f32)),
        grid=(d, n // BLK),
        in_specs=[spec(0, False), spec(1, True), spec(1, False), spec(2, True), spec(2, False),
                  _const_spec((A_HEADS, BLK, 2 * BLK))],
        out_specs=(pl.BlockSpec((BLK, width), lambda r, b: (b, r)),
                   pl.BlockSpec((BLK, width), lambda r, b: (b, r))),
        compiler_params=_params(("parallel", "parallel")),
        name=f"band_a{g}",
    )(view, view, view, view, view, bias)
    return out.reshape(s_len, width), lse.reshape(s_len, width)


def _band_attend_b(qkv, bias, sinks):
    s_len = qkv.shape[0]
    assert s_len % BLK == 0
    qw = B_Q_COLS // 2
    q_blk = A_COLS // qw
    k_blk = (A_COLS + B_Q_COLS) // B_KV_COLS
    head_cfg = tuple((h // 4, (h % 4) * HEAD_DIM, (h // 4) * HEAD_DIM) for h in range(B_Q_HEADS))

    def kv_spec(off, prev):
        if prev:
            return pl.BlockSpec((BLK, B_KV_COLS), lambda r, b: (jnp.maximum(b - 1, 0), k_blk + off))
        return pl.BlockSpec((BLK, B_KV_COLS), lambda r, b: (b, k_blk + off))

    return pl.pallas_call(
        functools.partial(_band_kernel, n_q=2, head_cfg=head_cfg, has_sink=True, want_lse=False),
        out_shape=jax.ShapeDtypeStruct((s_len, B_Q_COLS), f32),
        grid=(1, s_len // BLK),
        in_specs=[pl.BlockSpec((BLK, qw), lambda r, b: (b, q_blk)),
                  pl.BlockSpec((BLK, qw), lambda r, b: (b, q_blk + 1)),
                  kv_spec(0, True), kv_spec(0, False), kv_spec(1, True), kv_spec(1, False),
                  _const_spec((B_Q_HEADS, BLK, 2 * BLK)),
                  pl.BlockSpec(memory_space=pltpu.SMEM)],
        out_specs=pl.BlockSpec((BLK, B_Q_COLS), lambda r, b: (b, 0)),
        compiler_params=_params(("parallel", "parallel")),
        name="band_b",
    )(qkv, qkv, qkv, qkv, qkv, qkv, bias, sinks)


def _sample_kernel(*refs, n_q, head_cfg, has_sink, want_lse, kv_cols):
    q_refs = refs[:n_q]
    kn_ref, vn_ref, buf_ref, bo_ref, bn_ref = refs[n_q:n_q + 5]
    pos = n_q + 5
    sink_ref = None
    if has_sink:
        sink_ref = refs[pos]
        pos += 1
    o_ref = refs[pos]
    pos += 1
    l_ref = None
    if want_lse:
        l_ref = refs[pos]
        pos += 1
    nb_ref = refs[pos]

    buf_len = buf_ref.shape[1]
    t_new = DEC_SEQ
    k_new = kn_ref[...]
    v_new = vn_ref[...]

    nb_ref[0, 0:buf_len - t_new, :] = buf_ref[0, t_new:buf_len, :]
    nb_ref[0, buf_len - t_new:buf_len, 0:kv_cols] = k_new
    nb_ref[0, buf_len - t_new:buf_len, kv_cols:2 * kv_cols] = v_new

    rows = []
    for (qi, qoff, kvoff) in head_cfg:
        q = q_refs[qi][:, qoff:qoff + HEAD_DIM]
        pieces = []
        if kvoff > 0:
            pieces.append(jnp.zeros((t_new, kvoff), f32))
        pieces.append(q)
        if kv_cols - kvoff - HEAD_DIM > 0:
            pieces.append(jnp.zeros((t_new, kv_cols - kvoff - HEAD_DIM), f32))
        rows.append(jnp.concatenate(pieces, axis=1) if len(pieces) > 1 else q)
    qrows = jnp.concatenate(rows, axis=0).astype(bf16)

    pad = jnp.zeros((LANES - t_new, kv_cols), f32)
    k_old = buf_ref[0, :, 0:kv_cols].astype(bf16)
    v_old = buf_ref[0, :, kv_cols:2 * kv_cols].astype(bf16)
    k_pad = jnp.concatenate([k_new, pad], axis=0).astype(bf16)
    v_pad = jnp.concatenate([v_new, pad], axis=0).astype(bf16)

    contract_last = (((1,), (1,)), ((), ()))
    s_old = lax.dot_general(qrows, k_old, contract_last, preferred_element_type=f32) * SCALE + bo_ref[...]
    s_new = lax.dot_general(qrows, k_pad, contract_last, preferred_element_type=f32) * SCALE + bn_ref[...]
    m = jnp.maximum(jnp.max(s_old, axis=-1, keepdims=True), jnp.max(s_new, axis=-1, keepdims=True))
    n_heads = len(head_cfg)
    sink_col = None
    if has_sink:
        sink_col = jnp.concatenate(
            [jnp.full((t_new, 1), sink_ref[h], f32) for h in range(n_heads)], axis=0)
        m = jnp.maximum(m, sink_col)
    e_old = jnp.exp(s_old - m)
    e_new = jnp.exp(s_new - m)
    den = jnp.sum(e_old, axis=-1, keepdims=True) + jnp.sum(e_new, axis=-1, keepdims=True)
    if has_sink:
        den = den + jnp.exp(sink_col - m)
    acc = (jnp.dot((e_old / den).astype(bf16), v_old, preferred_element_type=f32)
           + jnp.dot((e_new / den).astype(bf16), v_pad, preferred_element_type=f32))
    lse = m + jnp.log(den)
    for h, (qi, qoff, kvoff) in enumerate(head_cfg):
        o_ref[:, h * HEAD_DIM:(h + 1) * HEAD_DIM] = acc[h * t_new:(h + 1) * t_new, kvoff:kvoff + HEAD_DIM]
        if want_lse:
            l_ref[:, h * HEAD_DIM:(h + 1) * HEAD_DIM] = jnp.broadcast_to(
                lse[h * t_new:(h + 1) * t_new, :], (t_new, HEAD_DIM))


def _sample_attend(qkv, buf, q_blocks, k_block, v_block, kv_cols, head_cfg, bias_old, bias_new, sinks, want_lse, name):
    bd, buf_len, width2 = buf.shape
    assert width2 == 2 * kv_cols and qkv.shape[0] == bd * DEC_SEQ
    n_heads = len(head_cfg)
    rows = n_heads * DEC_SEQ
    has_sink = sinks is not None

    def col_spec(width, idx):
        return pl.BlockSpec((DEC_SEQ, width), lambda b: (b, idx))

    in_specs = [col_spec(w, i) for (w, i) in q_blocks]
    in_specs += [col_spec(*k_block), col_spec(*v_block),
                 pl.BlockSpec((1, buf_len, width2), lambda b: (b, 0, 0)),
                 _const_spec((rows, buf_len)), _const_spec((rows, LANES))]
    args = [qkv] * (len(q_blocks) + 2) + [buf, bias_old, bias_new]
    if has_sink:
        in_specs.append(pl.BlockSpec(memory_space=pltpu.SMEM))
        args.append(sinks)
    out_w = n_heads * HEAD_DIM
    out_shape = [jax.ShapeDtypeStruct((bd * DEC_SEQ, out_w), f32)]
    out_specs = [pl.BlockSpec((DEC_SEQ, out_w), lambda b: (b, 0))]
    if want_lse:
        out_shape.append(jax.ShapeDtypeStruct((bd * DEC_SEQ, out_w), f32))
        out_specs.append(pl.BlockSpec((DEC_SEQ, out_w), lambda b: (b, 0)))
    out_shape.append(jax.ShapeDtypeStruct(buf.shape, f32))
    out_specs.append(pl.BlockSpec((1, buf_len, width2), lambda b: (b, 0, 0)))
    return pl.pallas_call(
        functools.partial(_sample_kernel, n_q=len(q_blocks), head_cfg=head_cfg, has_sink=has_sink,
                          want_lse=want_lse, kv_cols=kv_cols),
        out_shape=tuple(out_shape),
        grid=(bd,),
        in_specs=in_specs,
        out_specs=tuple(out_specs),
        compiler_params=_params(("parallel",)),
        name=name,
    )(*args)


def _finish_kernel(x_ref, oa0_ref, oa1_ref, oa2_ref, la0_ref, la1_ref, la2_ref, ob_ref, gate_ref,
                   wa_ref, wb_ref, wo_ref, g2_ref, h_ref, hnt_ref):
    l0, l1, l2 = la0_ref[...], la1_ref[...], la2_ref[...]
    m = jnp.maximum(jnp.maximum(l0, l1), l2)
    e0, e1, e2 = jnp.exp(l0 - m), jnp.exp(l1 - m), jnp.exp(l2 - m)
    den = e0 + e1 + e2
    oa = (e0 / den) * oa0_ref[...] + (e1 / den) * oa1_ref[...] + (e2 / den) * oa2_ref[...]
    ya = jnp.dot(oa.astype(bf16), wa_ref[...], preferred_element_type=f32)
    yb = jnp.dot(ob_ref[...].astype(bf16), wb_ref[...], preferred_element_type=f32)
    merged = gate_ref[:, :D_MODEL] * ya + gate_ref[:, D_MODEL:] * yb
    h = x_ref[...] + jnp.dot(merged.astype(bf16), wo_ref[...], preferred_element_type=f32)
    h_ref[...] = h
    ms = jnp.mean(h * h, axis=-1, keepdims=True)
    hn = (h * lax.rsqrt(ms + EPS)) * g2_ref[...]
    hnt_ref[...] = hn.T.astype(bf16)


def _finish(x, outs_a, lses_a, out_b, gates, wa_bf, wb_bf, wo_bf, norm2):
    n = x.shape[0]
    tm = PROJ_TM
    assert n % tm == 0
    aw = A_HEADS * HEAD_DIM

    def row_spec(w):
        return pl.BlockSpec((tm, w), lambda i: (i, 0))

    return pl.pallas_call(
        _finish_kernel,
        out_shape=(jax.ShapeDtypeStruct((n, D_MODEL), f32), jax.ShapeDtypeStruct((D_MODEL, n), bf16)),
        grid=(n // tm,),
        in_specs=[row_spec(D_MODEL)] + [row_spec(aw)] * 6 + [row_spec(B_Q_COLS), row_spec(GATE_COLS),
                  _const_spec((aw, D_MODEL)), _const_spec((B_Q_COLS, D_MODEL)), _const_spec((D_MODEL, D_MODEL)),
                  _const_spec((1, D_MODEL))],
        out_specs=(row_spec(D_MODEL), pl.BlockSpec((D_MODEL, tm), lambda i: (0, i))),
        compiler_params=_params(("parallel",)),
        name="finish",
    )(x, *outs_a, *lses_a, out_b, gates, wa_bf, wb_bf, wo_bf, norm2.reshape(1, D_MODEL))


def _batcher_pairs(n):
    pairs = []
    p = 1
    while p < n:
        k = p
        while k >= 1:
            for j in range(k % p, n - k, 2 * k):
                for i in range(min(k, n - j - k)):
                    if (i + j) // (2 * p) == (i + j + k) // (2 * p):
                        pairs.append((i + j, i + j + k))
            k //= 2
        p *= 2
    return pairs


_SORT16 = _batcher_pairs(PEER_TOPK)
_BITONIC16 = [(i, i + dist) for dist in (8, 4, 2, 1) for i in range(PEER_TOPK) if not i & dist]


def _exchange(xs, i, j):
    a, b = xs[i], xs[j]
    if b is None:
        return
    if a is None:
        xs[i], xs[j] = b, None
        return
    xs[i], xs[j] = jnp.maximum(a, b), jnp.minimum(a, b)


def _sort16_desc(xs):
    xs = list(xs)
    for i, j in _SORT16:
        _exchange(xs, i, j)
    return xs


def _merge_top16(xs, ys):
    zs = []
    for j in range(PEER_TOPK):
        a, b = xs[j], ys[PEER_TOPK - 1 - j]
        zs.append(b if a is None else a if b is None else jnp.maximum(a, b))
    for i, j in _BITONIC16:
        _exchange(zs, i, j)
    return zs


_CAND_PAIRS = [(a, b) for a in range(PEER_TOPK) for b in range(PEER_TOPK) if (a + 1) * (b + 1) <= PEER_TOPK]


def _peer_route(hnt_ref, wq_ref, sk1_ref, sk2_ref, s1_ref, s2_ref, e1_ref, e2_ref, tau_ref, top1_ref, top2_ref):
    hnt = hnt_ref[...]
    for half, (sk_ref, s_ref, top_ref) in enumerate(((sk1_ref, s1_ref, top1_ref), (sk2_ref, s2_ref, top2_ref))):
        for h in range(PEER_HEADS):
            r0 = (2 * h + half) * PEER_HALF
            qh = jnp.dot(wq_ref[r0:r0 + PEER_HALF, :], hnt, preferred_element_type=f32).astype(bf16)
            st = jnp.dot(sk_ref[...], qh, preferred_element_type=f32)
            s_ref[h] = st
            xs = _sort16_desc([st[j * SUBLANES:(j + 1) * SUBLANES, :] for j in range(N_SUBKEYS // SUBLANES)])
            for shift in (4, 2, 1):
                xs = _merge_top16(xs, [pltpu.roll(x, shift, 0) for x in xs])
            for a in range(PEER_TOPK):
                top_ref[a, h:h + 1, :] = xs[a][0:1, :]
    v1 = [top1_ref[a] for a in range(PEER_TOPK)]
    v2 = [top2_ref[a] for a in range(PEER_TOPK)]
    cands = [v1[a] + v2[b] for a, b in _CAND_PAIRS]
    cands += [None] * (-len(cands) % PEER_TOPK)
    groups = [_sort16_desc(cands[i::len(cands) // PEER_TOPK]) for i in range(len(cands) // PEER_TOPK)]
    while len(groups) > 1:
        groups = [_merge_top16(groups[i], groups[i + 1]) for i in range(0, len(groups), 2)]
    sc = groups[0]
    den = jnp.zeros_like(sc[0])
    for k in range(PEER_TOPK):
        den = den + jnp.exp(sc[k] - sc[0])
    inv_den = 1.0 / den
    tau_ref[...] = sc[PEER_TOPK - 1]
    for h in range(PEER_HEADS):
        e1_ref[h] = jnp.exp(s1_ref[h] - v1[0][h:h + 1, :]) * inv_den[h:h + 1, :]
        e2_ref[h] = jnp.exp(s2_ref[h] - v2[0][h:h + 1, :])


def _gelu_exact(x):
    return 0.5 * x * (1.0 + lax.erf(x * (2.0 ** -0.5)))


def _peer_kernel(hnt_ref, h_ref, wq_ref, sk1_ref, sk2_ref, u_ref, vt_ref, y_ref,
                 s1_ref, s2_ref, e1_ref, e2_ref, tau_ref, top1_ref, top2_ref, act_ref, acc_ref):
    eb = pl.program_id(1)
    blocks = PEER_EB // N_SUBKEYS

    @pl.when(eb == 0)
    def _():
        _peer_route(hnt_ref, wq_ref, sk1_ref, sk2_ref, s1_ref, s2_ref, e1_ref, e2_ref, tau_ref, top1_ref, top2_ref)
        acc_ref[...] = jnp.zeros_like(acc_ref)

    def body(i, carry):
        i1 = eb * blocks + i
        rows = pl.ds(pl.multiple_of(i * N_SUBKEYS, N_SUBKEYS), N_SUBKEYS)
        hu = jnp.dot(u_ref[rows, :], hnt_ref[...], preferred_element_type=f32)
        for c in range(PEER_T // PEER_TC):
            tok = slice(c * PEER_TC, (c + 1) * PEER_TC)
            g = jnp.zeros((N_SUBKEYS, PEER_TC), f32)
            for h in range(PEER_HEADS):
                pair = s1_ref[h, pl.ds(i1, 1), tok] + s2_ref[h, :, tok]
                w = e1_ref[h, pl.ds(i1, 1), tok] * e2_ref[h, :, tok]
                g = g + jnp.where(pair >= tau_ref[h:h + 1, tok], w, 0.0)
            act = g * _gelu_exact(hu[:, tok])
            act_ref[rows, tok] = act.astype(bf16)
        return carry

    lax.fori_loop(0, blocks, body, 0)
    acc_ref[...] += jnp.dot(vt_ref[...], act_ref[...], preferred_element_type=f32)

    @pl.when(eb == pl.num_programs(1) - 1)
    def _():
        y_ref[...] = h_ref[...] + acc_ref[...].T


def _peer(hnt, h, wq_t, sk1, sk2, u_bf, vt_bf):
    n = h.shape[0]
    t = PEER_T
    assert n % t == 0
    return pl.pallas_call(
        _peer_kernel,
        out_shape=jax.ShapeDtypeStruct((n, D_MODEL), f32),
        grid=(n // t, N_EXPERTS // PEER_EB),
        in_specs=[pl.BlockSpec((D_MODEL, t), lambda j, e: (0, j)),
                  pl.BlockSpec((t, D_MODEL), lambda j, e: (j, 0)),
                  _const_spec((2 * PEER_HEADS * PEER_HALF, D_MODEL)),
                  _const_spec((N_SUBKEYS, PEER_HALF)), _const_spec((N_SUBKEYS, PEER_HALF)),
                  pl.BlockSpec((PEER_EB, D_MODEL), lambda j, e: (e, 0)),
                  pl.BlockSpec((D_MODEL, PEER_EB), lambda j, e: (0, e))],
        out_specs=pl.BlockSpec((t, D_MODEL), lambda j, e: (j, 0)),
        scratch_shapes=[pltpu.VMEM((PEER_HEADS, N_SUBKEYS, t), f32)] * 4 + [
            pltpu.VMEM((PEER_HEADS, t), f32),
            pltpu.VMEM((PEER_TOPK, PEER_HEADS, t), f32),
            pltpu.VMEM((PEER_TOPK, PEER_HEADS, t), f32),
            pltpu.VMEM((PEER_EB, t), bf16),
            pltpu.VMEM((D_MODEL, t), f32)],
        compiler_params=_params(("parallel", "arbitrary")),
        name="peer",
    )(hnt, h, wq_t, sk1, sk2, u_bf, vt_bf)


def kernel(x_prompt, x_sample, cache_a1_kv, cache_a2_kv, cache_a3_kv, cache_b_kv, norm1, w_in, q_norm_a, k_norm_a,
           q_norm_b, k_norm_b, rel_bias_table, sinks_b, w_up_a, w_up_b, w_out, norm2, w_peer_q, sub_keys_1,
           sub_keys_2, expert_u, expert_v):
    batch, seq, _ = x_prompt.shape
    dec_batch, dec_seq, _ = x_sample.shape
    assert batch == 1 and dec_seq == DEC_SEQ

    w_in_bf = w_in.astype(bf16)
    wa_bf, wb_bf, wo_bf = w_up_a.astype(bf16), w_up_b.astype(bf16), w_out.astype(bf16)
    wq_t = w_peer_q.T.astype(bf16)
    sk1, sk2 = sub_keys_1.astype(bf16), sub_keys_2.astype(bf16)
    u_bf = expert_u.astype(bf16)
    vt_bf = expert_v.T.astype(bf16)
    ones_v = jnp.ones((A_HEADS * HEAD_DIM,), f32)
    gain = jnp.concatenate(
        [jnp.concatenate([jnp.tile(q_norm_a[g], A_HEADS), jnp.tile(k_norm_a[g], A_HEADS), ones_v])
         for g in range(N_A_GROUPS)]
        + [jnp.tile(q_norm_b, B_Q_HEADS), jnp.tile(k_norm_b, B_KV_HEADS), jnp.ones((B_KV_COLS,), f32)]
    ).astype(f32).reshape(1, QKV_COLS)
    lane_head = jnp.arange(MXU_COLS) // HEAD_DIM
    seg = (lane_head[:, None] == lane_head[None, :]).astype(bf16)
    b_heads = slice(N_A_GROUPS * A_HEADS, N_A_GROUPS * A_HEADS + B_Q_HEADS)
    tables_a = [rel_bias_table[:, g * A_HEADS:(g + 1) * A_HEADS] for g in range(N_A_GROUPS)]
    table_b = rel_bias_table[:, b_heads]
    sinks = sinks_b.astype(f32)

    xp = x_prompt.reshape(seq, D_MODEL)
    qkv_p, gates_p = _project(xp, norm1, w_in_bf, gain, seg)
    outs_p, lses_p = [], []
    for g, d in enumerate(A_DILATIONS):
        o, l = _band_attend_a(qkv_p, g, d, _band_bias(tables_a[g], d, A_SPAN))
        outs_p.append(o)
        lses_p.append(l)
    out_b_p = _band_attend_b(qkv_p, _band_bias(table_b, 1, B_WINDOW), sinks)
    h_p, hnt_p = _finish(xp, outs_p, lses_p, out_b_p, gates_p, wa_bf, wb_bf, wo_bf, norm2)
    y_prompt = _peer(hnt_p, h_p, wq_t, sk1, sk2, u_bf, vt_bf).reshape(batch, seq, D_MODEL)

    new_p = []
    for g, d in enumerate(A_DILATIONS):
        lp = min(A_SPAN * d, seq)
        c0 = g * A_GROUP_COLS + A_HEADS * HEAD_DIM
        new_p.append(qkv_p[seq - lp:, c0:c0 + 2 * A_HEADS * HEAD_DIM].reshape(batch, lp, 2, A_HEADS, HEAD_DIM))
    lpb = min(B_WINDOW, seq)
    new_b_prompt = qkv_p[seq - lpb:, A_COLS + B_Q_COLS:].reshape(batch, lpb, 2, B_KV_HEADS, HEAD_DIM)

    xs = x_sample.reshape(dec_batch * dec_seq, D_MODEL)
    qkv_s, gates_s = _project(xs, norm1, w_in_bf, gain, seg)
    aw = A_HEADS * HEAD_DIM
    outs_s, lses_s, new_s = [], [], []
    cfg_a = tuple((0, h * HEAD_DIM, h * HEAD_DIM) for h in range(A_HEADS))
    for g, (d, buf) in enumerate(zip(A_DILATIONS, (cache_a1_kv, cache_a2_kv, cache_a3_kv))):
        buf_len = buf.shape[1]
        bo, bn = _sample_bias(tables_a[g], d, A_SPAN, buf_len)
        o, l, nb = _sample_attend(qkv_s, buf.reshape(dec_batch, buf_len, 2 * aw),
                                  [(aw, 3 * g)], (aw, 3 * g + 1), (aw, 3 * g + 2), aw, cfg_a, bo, bn,
                                  None, True, f"sample_a{g}")
        outs_s.append(o)
        lses_s.append(l)
        new_s.append(nb.reshape(buf.shape))
    buf_len = cache_b_kv.shape[1]
    bo, bn = _sample_bias(table_b, 1, B_WINDOW, buf_len)
    cfg_b = tuple((h // 4, (h % 4) * HEAD_DIM, (h // 4) * HEAD_DIM) for h in range(B_Q_HEADS))
    qw = B_Q_COLS // 2
    k_blk = (A_COLS + B_Q_COLS) // B_KV_COLS
    out_b_s, new_b = _sample_attend(qkv_s, cache_b_kv.reshape(dec_batch, buf_len, 2 * B_KV_COLS),
                                    [(qw, A_COLS // qw), (qw, A_COLS // qw + 1)], (B_KV_COLS, k_blk),
                                    (B_KV_COLS, k_blk + 1), B_KV_COLS, cfg_b, bo, bn, sinks, False, "sample_b")
    new_b_sample = new_b.reshape(cache_b_kv.shape)
    h_s, hnt_s = _finish(xs, outs_s, lses_s, out_b_s, gates_s, wa_bf, wb_bf, wo_bf, norm2)
    y_sample = _peer(hnt_s, h_s, wq_t, sk1, sk2, u_bf, vt_bf).reshape(dec_batch, dec_seq, D_MODEL)

    return (y_prompt, y_sample, new_p[0], new_p[1], new_p[2], new_b_prompt,
            new_s[0], new_s[1], new_s[2], new_b_sample)
```

```python
import functools
import math

import jax
import jax.numpy as jnp
from jax import lax
from jax.experimental import pallas as pl
from jax.experimental.pallas import tpu as pltpu

f32 = jnp.float32
bf16 = jnp.bfloat16

D_MODEL = 1024
HEAD_DIM = 64
A_SPAN = 128
A_DILATIONS = (1, 4, 16)
A_HEADS = 4
N_A_GROUPS = 3
B_WINDOW = 128
B_Q_HEADS = 8
B_KV_HEADS = 2
N_BUCKETS = 32
MAX_DISTANCE = 2048
BLK = 128
DEC_SEQ = 8
A_GROUP_COLS = 3 * A_HEADS * HEAD_DIM
A_COLS = N_A_GROUPS * A_GROUP_COLS
B_Q_COLS = B_Q_HEADS * HEAD_DIM
B_KV_COLS = B_KV_HEADS * HEAD_DIM
QKV_COLS = A_COLS + B_Q_COLS + 2 * B_KV_COLS
GATE_COLS = 2 * D_MODEL
N_SUBKEYS = 128
N_EXPERTS = N_SUBKEYS * N_SUBKEYS
PEER_HEADS = 8
PEER_TOPK = 16
PEER_HALF = 128
EPS = 1e-6
NEG = -1e30
SCALE = HEAD_DIM ** -0.5

LANES = 128
SUBLANES = 8
MXU_COLS = 256
VMEM_LIMIT = 56 * 1024 * 1024

PROJ_TM = 256
PEER_T = 512
PEER_EB = 2048
PEER_SUB = 512
PEER_ROWS = 64
assert (PEER_EB // N_SUBKEYS) % SUBLANES == 0 and PEER_EB % PEER_SUB == 0 and PEER_SUB % N_SUBKEYS == 0


def _const_spec(shape):
    nd = len(shape)
    return pl.BlockSpec(shape, lambda *_: (0,) * nd, pipeline_mode=pl.Buffered(1))


def _params(sem):
    return pltpu.CompilerParams(dimension_semantics=sem, vmem_limit_bytes=VMEM_LIMIT)


_NORM_ALL, _NORM_NONE, _NORM_FIRST_HALF = 0, 1, 2


def _chunk_norm_mode(c):
    col = c * MXU_COLS
    if col < A_COLS:
        return _NORM_NONE if (col % A_GROUP_COLS) == 2 * A_HEADS * HEAD_DIM else _NORM_ALL
    if col < A_COLS + B_Q_COLS:
        return _NORM_ALL
    return _NORM_FIRST_HALF


def _proj_kernel(x_ref, g1_ref, w_ref, gain_ref, seg_ref, qkv_ref, gate_ref):
    x = x_ref[...]
    ms = jnp.mean(x * x, axis=-1, keepdims=True)
    xn = ((x * lax.rsqrt(ms + EPS)) * g1_ref[...]).astype(bf16)
    seg = seg_ref[...]
    for c in range(QKV_COLS // MXU_COLS):
        cols = slice(c * MXU_COLS, (c + 1) * MXU_COLS)
        hb = jnp.dot(xn, w_ref[:, cols], preferred_element_type=f32)
        mode = _chunk_norm_mode(c)
        if mode != _NORM_NONE:
            h2 = hb * hb
            hi = h2.astype(bf16)
            lo = (h2 - hi.astype(f32)).astype(bf16)
            ss = (jnp.dot(hi, seg, preferred_element_type=f32)
                  + jnp.dot(lo, seg, preferred_element_type=f32))
            normed = (hb * lax.rsqrt(ss * (1.0 / HEAD_DIM) + EPS)) * gain_ref[:, cols]
            if mode == _NORM_ALL:
                hb = normed
            else:
                hb = jnp.concatenate([normed[:, :LANES], hb[:, LANES:]], axis=1)
        qkv_ref[:, cols] = hb
    hg = jnp.dot(xn, w_ref[:, QKV_COLS:], preferred_element_type=f32)
    gate_ref[...] = jax.nn.sigmoid(hg)


def _project(x, norm1, w_in_bf, gain, seg):
    n = x.shape[0]
    tm = PROJ_TM
    assert n % tm == 0
    return pl.pallas_call(
        _proj_kernel,
        out_shape=(jax.ShapeDtypeStruct((n, QKV_COLS), f32), jax.ShapeDtypeStruct((n, GATE_COLS), f32)),
        grid=(n // tm,),
        in_specs=[
            pl.BlockSpec((tm, D_MODEL), lambda i: (i, 0)),
            _const_spec((1, D_MODEL)),
            _const_spec((D_MODEL, QKV_COLS + GATE_COLS)),
            _const_spec((1, QKV_COLS)),
            _const_spec((MXU_COLS, MXU_COLS)),
        ],
        out_specs=(pl.BlockSpec((tm, QKV_COLS), lambda i: (i, 0)),
                   pl.BlockSpec((tm, GATE_COLS), lambda i: (i, 0))),
        compiler_params=_params(("parallel",)),
        name="project",
    )(x, norm1.reshape(1, D_MODEL), w_in_bf, gain, seg)


def _t5_bucket(dist):
    exact = N_BUCKETS // 2
    d32 = jnp.maximum(dist, 1).astype(f32)
    large = exact + (jnp.log(d32 / exact) / math.log(MAX_DISTANCE / exact) * (N_BUCKETS - exact)).astype(jnp.int32)
    large = jnp.minimum(large, N_BUCKETS - 1)
    return jnp.where(dist < exact, dist, large)


def _toeplitz(period_vals, rows, width):
    reps = -(-rows * width // (width + 1))
    return jnp.tile(period_vals, (1, reps))[:, :rows * width].reshape(-1, rows, width)


def _band_bias(table_h, d, span):
    k = jnp.arange(2 * BLK + 1)
    steps = BLK - k
    vals = table_h[_t5_bucket(jnp.maximum(steps, 0) * d)].astype(f32).T
    vals = jnp.where(((steps >= 0) & (steps <= span))[None], vals, NEG)
    return _toeplitz(vals, BLK, 2 * BLK)


def _sample_bias(table_h, d, steps, buf_len):
    assert buf_len == steps * d
    width = buf_len + DEC_SEQ
    dist = buf_len - jnp.arange(width + 1)
    ok = (dist >= 0) & (dist % d == 0)
    vals = table_h[_t5_bucket(jnp.maximum(dist, 0))].astype(f32).T
    vals = jnp.where(ok[None], vals, NEG)
    b = _toeplitz(vals, DEC_SEQ, width).reshape(-1, width)
    old = b[:, :buf_len]
    new = jnp.pad(b[:, buf_len:], ((0, 0), (0, LANES - DEC_SEQ)), constant_values=NEG)
    return old, new


def _band_kernel(*refs, n_q, head_cfg, has_sink, want_lse):
    q_refs = refs[:n_q]
    kp_ref, kc_ref, vp_ref, vc_ref, bias_ref = refs[n_q:n_q + 5]
    pos = n_q + 5
    sink_ref = None
    if has_sink:
        sink_ref = refs[pos]
        pos += 1
    o_ref = refs[pos]
    l_ref = refs[pos + 1] if want_lse else None

    blk = pl.program_id(1)
    col = lax.broadcasted_iota(jnp.int32, (BLK, 2 * BLK), 1)
    no_prev = jnp.where((col < BLK) & (blk == 0), NEG, 0.0).astype(f32)
    contract_last = (((1,), (1,)), ((), ()))
    for h, (qi, qoff, kvoff) in enumerate(head_cfg):
        q = q_refs[qi][:, qoff:qoff + HEAD_DIM].astype(bf16)
        k = jnp.concatenate([kp_ref[:, kvoff:kvoff + HEAD_DIM], kc_ref[:, kvoff:kvoff + HEAD_DIM]], axis=0).astype(bf16)
        v = jnp.concatenate([vp_ref[:, kvoff:kvoff + HEAD_DIM], vc_ref[:, kvoff:kvoff + HEAD_DIM]], axis=0).astype(bf16)
        s = lax.dot_general(q, k, contract_last, preferred_element_type=f32) * SCALE + bias_ref[h] + no_prev
        m = jnp.max(s, axis=-1, keepdims=True)
        if has_sink:
            sink = sink_ref[h]
            m = jnp.maximum(m, sink)
        e = jnp.exp(s - m)
        den = jnp.sum(e, axis=-1, keepdims=True)
        if has_sink:
            den = den + jnp.exp(sink - m)
        p = (e / den).astype(bf16)
        o_ref[:, h * HEAD_DIM:(h + 1) * HEAD_DIM] = jnp.dot(p, v, preferred_element_type=f32)
        if want_lse:
            l_ref[:, h * HEAD_DIM:(h + 1) * HEAD_DIM] = jnp.broadcast_to(m + jnp.log(den), (BLK, HEAD_DIM))


def _band_attend_a(qkv, g, d, bias):
    s_len = qkv.shape[0]
    assert s_len % (d * BLK) == 0
    n = s_len // d
    width = A_HEADS * HEAD_DIM
    per_tok = QKV_COLS // width
    view = qkv.reshape(n, d * QKV_COLS)
    base = g * 3

    def spec(off, prev):
        if prev:
            return pl.BlockSpec((BLK, width), lambda r, b: (jnp.maximum(b - 1, 0), r * per_tok + base + off))
        return pl.BlockSpec((BLK, width), lambda r, b: (b, r * per_tok + base + off))

    head_cfg = tuple((0, h * HEAD_DIM, h * HEAD_DIM) for h in range(A_HEADS))
    out, lse = pl.pallas_call(
        functools.partial(_band_kernel, n_q=1, head_cfg=head_cfg, has_sink=False, want_lse=True),
        out_shape=(jax.ShapeDtypeStruct((n, d * width), f32), jax.ShapeDtypeStruct((n, d * width), f32)),
        grid=(d, n // BLK),
        in_specs=[spec(0, False), spec(1, True), spec(1, False), spec(2, True), spec(2, False),
                  _const_spec((A_HEADS, BLK, 2 * BLK))],
        out_specs=(pl.BlockSpec((BLK, width), lambda r, b: (b, r)),
                   pl.BlockSpec((BLK, width), lambda r, b: (b, r))),
        compiler_params=_params(("parallel", "parallel")),
        name=f"band_a{g}",
    )(view, view, view, view, view, bias)
    return out.reshape(s_len, width), lse.reshape(s_len, width)


def _band_attend_b(qkv, bias, sinks):
    s_len = qkv.shape[0]
    assert s_len % BLK == 0
    qw = B_Q_COLS // 2
    q_blk = A_COLS // qw
    k_blk = (A_COLS + B_Q_COLS) // B_KV_COLS
    head_cfg = tuple((h // 4, (h % 4) * HEAD_DIM, (h // 4) * HEAD_DIM) for h in range(B_Q_HEADS))

    def kv_spec(off, prev):
        if prev:
            return pl.BlockSpec((BLK, B_KV_COLS), lambda r, b: (jnp.maximum(b - 1, 0), k_blk + off))
        return pl.BlockSpec((BLK, B_KV_COLS), lambda r, b: (b, k_blk + off))

    return pl.pallas_call(
        functools.partial(_band_kernel, n_q=2, head_cfg=head_cfg, has_sink=True, want_lse=False),
        out_shape=jax.ShapeDtypeStruct((s_len, B_Q_COLS), f32),
        grid=(1, s_len // BLK),
        in_specs=[pl.BlockSpec((BLK, qw), lambda r, b: (b, q_blk)),
                  pl.BlockSpec((BLK, qw), lambda r, b: (b, q_blk + 1)),
                  kv_spec(0, True), kv_spec(0, False), kv_spec(1, True), kv_spec(1, False),
                  _const_spec((B_Q_HEADS, BLK, 2 * BLK)),
                  pl.BlockSpec(memory_space=pltpu.SMEM)],
        out_specs=pl.BlockSpec((BLK, B_Q_COLS), lambda r, b: (b, 0)),
        compiler_params=_params(("parallel", "parallel")),
        name="band_b",
    )(qkv, qkv, qkv, qkv, qkv, qkv, bias, sinks)


def _sample_kernel(*refs, n_q, head_cfg, has_sink, want_lse, kv_cols):
    q_refs = refs[:n_q]
    kn_ref, vn_ref, buf_ref, bo_ref, bn_ref = refs[n_q:n_q + 5]
    pos = n_q + 5
    sink_ref = None
    if has_sink:
        sink_ref = refs[pos]
        pos += 1
    o_ref = refs[pos]
    pos += 1
    l_ref = None
    if want_lse:
        l_ref = refs[pos]
        pos += 1
    nb_ref = refs[pos]

    buf_len = buf_ref.shape[1]
    t_new = DEC_SEQ
    k_new = kn_ref[...]
    v_new = vn_ref[...]

    nb_ref[0, 0:buf_len - t_new, :] = buf_ref[0, t_new:buf_len, :]
    nb_ref[0, buf_len - t_new:buf_len, 0:kv_cols] = k_new
    nb_ref[0, buf_len - t_new:buf_len, kv_cols:2 * kv_cols] = v_new

    rows = []
    for (qi, qoff, kvoff) in head_cfg:
        q = q_refs[qi][:, qoff:qoff + HEAD_DIM]
        pieces = []
        if kvoff > 0:
            pieces.append(jnp.zeros((t_new, kvoff), f32))
        pieces.append(q)
        if kv_cols - kvoff - HEAD_DIM > 0:
            pieces.append(jnp.zeros((t_new, kv_cols - kvoff - HEAD_DIM), f32))
        rows.append(jnp.concatenate(pieces, axis=1) if len(pieces) > 1 else q)
    qrows = jnp.concatenate(rows, axis=0).astype(bf16)

    pad = jnp.zeros((LANES - t_new, kv_cols), f32)
    k_old = buf_ref[0, :, 0:kv_cols].astype(bf16)
    v_old = buf_ref[0, :, kv_cols:2 * kv_cols].astype(bf16)
    k_pad = jnp.concatenate([k_new, pad], axis=0).astype(bf16)
    v_pad = jnp.concatenate([v_new, pad], axis=0).astype(bf16)

    contract_last = (((1,), (1,)), ((), ()))
    s_old = lax.dot_general(qrows, k_old, contract_last, preferred_element_type=f32) * SCALE + bo_ref[...]
    s_new = lax.dot_general(qrows, k_pad, contract_last, preferred_element_type=f32) * SCALE + bn_ref[...]
    m = jnp.maximum(jnp.max(s_old, axis=-1, keepdims=True), jnp.max(s_new, axis=-1, keepdims=True))
    n_heads = len(head_cfg)
    sink_col = None
    if has_sink:
        sink_col = jnp.concatenate(
            [jnp.full((t_new, 1), sink_ref[h], f32) for h in range(n_heads)], axis=0)
        m = jnp.maximum(m, sink_col)
    e_old = jnp.exp(s_old - m)
    e_new = jnp.exp(s_new - m)
    den = jnp.sum(e_old, axis=-1, keepdims=True) + jnp.sum(e_new, axis=-1, keepdims=True)
    if has_sink:
        den = den + jnp.exp(sink_col - m)
    acc = (jnp.dot((e_old / den).astype(bf16), v_old, preferred_element_type=f32)
           + jnp.dot((e_new / den).astype(bf16), v_pad, preferred_element_type=f32))
    lse = m + jnp.log(den)
    for h, (qi, qoff, kvoff) in enumerate(head_cfg):
        o_ref[:, h * HEAD_DIM:(h + 1) * HEAD_DIM] = acc[h * t_new:(h + 1) * t_new, kvoff:kvoff + HEAD_DIM]
        if want_lse:
            l_ref[:, h * HEAD_DIM:(h + 1) * HEAD_DIM] = jnp.broadcast_to(
                lse[h * t_new:(h + 1) * t_new, :], (t_new, HEAD_DIM))


def _sample_attend(qkv, buf, q_blocks, k_block, v_block, kv_cols, head_cfg, bias_old, bias_new, sinks, want_lse, name):
    bd, buf_len, width2 = buf.shape
    assert width2 == 2 * kv_cols and qkv.shape[0] == bd * DEC_SEQ
    n_heads = len(head_cfg)
    rows = n_heads * DEC_SEQ
    has_sink = sinks is not None

    def col_spec(width, idx):
        return pl.BlockSpec((DEC_SEQ, width), lambda b: (b, idx))

    in_specs = [col_spec(w, i) for (w, i) in q_blocks]
    in_specs += [col_spec(*k_block), col_spec(*v_block),
                 pl.BlockSpec((1, buf_len, width2), lambda b: (b, 0, 0)),
                 _const_spec((rows, buf_len)), _const_spec((rows, LANES))]
    args = [qkv] * (len(q_blocks) + 2) + [buf, bias_old, bias_new]
    if has_sink:
        in_specs.append(pl.BlockSpec(memory_space=pltpu.SMEM))
        args.append(sinks)
    out_w = n_heads * HEAD_DIM
    out_shape = [jax.ShapeDtypeStruct((bd * DEC_SEQ, out_w), f32)]
    out_specs = [pl.BlockSpec((DEC_SEQ, out_w), lambda b: (b, 0))]
    if want_lse:
        out_shape.append(jax.ShapeDtypeStruct((bd * DEC_SEQ, out_w), f32))
        out_specs.append(pl.BlockSpec((DEC_SEQ, out_w), lambda b: (b, 0)))
    out_shape.append(jax.ShapeDtypeStruct(buf.shape, f32))
    out_specs.append(pl.BlockSpec((1, buf_len, width2), lambda b: (b, 0, 0)))
    return pl.pallas_call(
        functools.partial(_sample_kernel, n_q=len(q_blocks), head_cfg=head_cfg, has_sink=has_sink,
                          want_lse=want_lse, kv_cols=kv_cols),
        out_shape=tuple(out_shape),
        grid=(bd,),
        in_specs=in_specs,
        out_specs=tuple(out_specs),
        compiler_params=_params(("parallel",)),
        name=name,
    )(*args)


def _finish_kernel(x_ref, oa0_ref, oa1_ref, oa2_ref, la0_ref, la1_ref, la2_ref, ob_ref, gate_ref,
                   wa_ref, wb_ref, wo_ref, g2_ref, h_ref, hnt_ref):
    l0, l1, l2 = la0_ref[...], la1_ref[...], la2_ref[...]
    m = jnp.maximum(jnp.maximum(l0, l1), l2)
    e0, e1, e2 = jnp.exp(l0 - m), jnp.exp(l1 - m), jnp.exp(l2 - m)
    den = e0 + e1 + e2
    oa = (e0 / den) * oa0_ref[...] + (e1 / den) * oa1_ref[...] + (e2 / den) * oa2_ref[...]
    ya = jnp.dot(oa.astype(bf16), wa_ref[...], preferred_element_type=f32)
    yb = jnp.dot(ob_ref[...].astype(bf16), wb_ref[...], preferred_element_type=f32)
    merged = gate_ref[:, :D_MODEL] * ya + gate_ref[:, D_MODEL:] * yb
    h = x_ref[...] + jnp.dot(merged.astype(bf16), wo_ref[...], preferred_element_type=f32)
    h_ref[...] = h
    ms = jnp.mean(h * h, axis=-1, keepdims=True)
    hn = (h * lax.rsqrt(ms + EPS)) * g2_ref[...]
    hnt_ref[...] = hn.T.astype(bf16)


def _finish(x, outs_a, lses_a, out_b, gates, wa_bf, wb_bf, wo_bf, norm2):
    n = x.shape[0]
    tm = PROJ_TM
    assert n % tm == 0
    aw = A_HEADS * HEAD_DIM

    def row_spec(w):
        return pl.BlockSpec((tm, w), lambda i: (i, 0))

    return pl.pallas_call(
        _finish_kernel,
        out_shape=(jax.ShapeDtypeStruct((n, D_MODEL), f32), jax.ShapeDtypeStruct((D_MODEL, n), bf16)),
        grid=(n // tm,),
        in_specs=[row_spec(D_MODEL)] + [row_spec(aw)] * 6 + [row_spec(B_Q_COLS), row_spec(GATE_COLS),
                  _const_spec((aw, D_MODEL)), _const_spec((B_Q_COLS, D_MODEL)), _const_spec((D_MODEL, D_MODEL)),
                  _const_spec((1, D_MODEL))],
        out_specs=(row_spec(D_MODEL), pl.BlockSpec((D_MODEL, tm), lambda i: (0, i))),
        compiler_params=_params(("parallel",)),
        name="finish",
    )(x, *outs_a, *lses_a, out_b, gates, wa_bf, wb_bf, wo_bf, norm2.reshape(1, D_MODEL))


def _batcher_pairs(n):
    pairs = []
    p = 1
    while p < n:
        k = p
        while k >= 1:
            for j in range(k % p, n - k, 2 * k):
                for i in range(min(k, n - j - k)):
                    if (i + j) // (2 * p) == (i + j + k) // (2 * p):
                        pairs.append((i + j, i + j + k))
            k //= 2
        p *= 2
    return pairs


_SORT16 = _batcher_pairs(PEER_TOPK)
_BITONIC16 = [(i, i + dist) for dist in (8, 4, 2, 1) for i in range(PEER_TOPK) if not i & dist]


def _exchange(xs, i, j):
    a, b = xs[i], xs[j]
    if b is None:
        return
    if a is None:
        xs[i], xs[j] = b, None
        return
    xs[i], xs[j] = jnp.maximum(a, b), jnp.minimum(a, b)


def _sort16_desc(xs):
    xs = list(xs)
    for i, j in _SORT16:
        _exchange(xs, i, j)
    return xs


def _merge_top16(xs, ys):
    zs = []
    for j in range(PEER_TOPK):
        a, b = xs[j], ys[PEER_TOPK - 1 - j]
        zs.append(b if a is None else a if b is None else jnp.maximum(a, b))
    for i, j in _BITONIC16:
        _exchange(zs, i, j)
    return zs


_CAND_PAIRS = [(a, b) for a in range(PEER_TOPK) for b in range(PEER_TOPK) if (a + 1) * (b + 1) <= PEER_TOPK]


def _peer_route(hnt_ref, wq_ref, sk1_ref, sk2_ref, s1_ref, s2_ref, e1_ref, e2_ref, tau_ref, top1_ref, top2_ref):
    hnt = hnt_ref[...]
    for half, (sk_ref, s_ref, top_ref) in enumerate(((sk1_ref, s1_ref, top1_ref), (sk2_ref, s2_ref, top2_ref))):
        for h in range(PEER_HEADS):
            r0 = (2 * h + half) * PEER_HALF
            qh = jnp.dot(wq_ref[r0:r0 + PEER_HALF, :], hnt, preferred_element_type=f32).astype(bf16)
            st = jnp.dot(sk_ref[...], qh, preferred_element_type=f32)
            s_ref[h] = st
            xs = _sort16_desc([st[j * SUBLANES:(j + 1) * SUBLANES, :] for j in range(N_SUBKEYS // SUBLANES)])
            for shift in (4, 2, 1):
                xs = _merge_top16(xs, [pltpu.roll(x, shift, 0) for x in xs])
            for a in range(PEER_TOPK):
                top_ref[a, h:h + 1, :] = xs[a][0:1, :]
    v1 = [top1_ref[a] for a in range(PEER_TOPK)]
    v2 = [top2_ref[a] for a in range(PEER_TOPK)]
    cands = [v1[a] + v2[b] for a, b in _CAND_PAIRS]
    cands += [None] * (-len(cands) % PEER_TOPK)
    groups = [_sort16_desc(cands[i::len(cands) // PEER_TOPK]) for i in range(len(cands) // PEER_TOPK)]
    while len(groups) > 1:
        groups = [_merge_top16(groups[i], groups[i + 1]) for i in range(0, len(groups), 2)]
    sc = groups[0]
    den = jnp.zeros_like(sc[0])
    for k in range(PEER_TOPK):
        den = den + jnp.exp(sc[k] - sc[0])
    inv_den = 1.0 / den
    tau_ref[...] = sc[PEER_TOPK - 1]
    for h in range(PEER_HEADS):
        e1_ref[h] = jnp.exp(s1_ref[h] - v1[0][h:h + 1, :]) * inv_den[h:h + 1, :]
        e2_ref[h] = jnp.exp(s2_ref[h] - v2[0][h:h + 1, :])


def _gelu_exact(x):
    return 0.5 * x * (1.0 + lax.erf(x * (2.0 ** -0.5)))


def _peer_kernel(hnt_ref, h_ref, wq_ref, sk1_ref, sk2_ref, u_ref, vt_ref, y_ref,
                 s1_ref, s2_ref, e1_ref, e2_ref, tau_ref, top1_ref, top2_ref, hu_ref, act_ref, acc_ref):
    eb = pl.program_id(1)
    n_sub = PEER_EB // PEER_SUB
    blocks = PEER_SUB // N_SUBKEYS

    @pl.when(eb == 0)
    def _():
        _peer_route(hnt_ref, wq_ref, sk1_ref, sk2_ref, s1_ref, s2_ref, e1_ref, e2_ref, tau_ref, top1_ref, top2_ref)
        acc_ref[...] = jnp.zeros_like(acc_ref)

    def expert_rows(k):
        return slice(k * PEER_SUB, (k + 1) * PEER_SUB)

    def project_sub(k):
        hu_ref[k % 2] = jnp.dot(u_ref[expert_rows(k), :], hnt_ref[...], preferred_element_type=f32)

    def activate_sub(k):
        for i in range(blocks):
            local = k * blocks + i
            group = pl.ds(pl.multiple_of(eb * (PEER_EB // N_SUBKEYS) + (local // SUBLANES) * SUBLANES, SUBLANES),
                          SUBLANES)
            sub = local % SUBLANES
            for c in range(PEER_T // LANES):
                tok = slice(c * LANES, (c + 1) * LANES)
                for part in range(N_SUBKEYS // PEER_ROWS):
                    i2 = slice(part * PEER_ROWS, (part + 1) * PEER_ROWS)
                    rows = slice(i * N_SUBKEYS + part * PEER_ROWS, i * N_SUBKEYS + (part + 1) * PEER_ROWS)
                    g = jnp.zeros((PEER_ROWS, LANES), f32)
                    for h in range(PEER_HEADS):
                        s1_row = s1_ref[h, group, tok][sub:sub + 1, :]
                        e1_row = e1_ref[h, group, tok][sub:sub + 1, :]
                        pair = s1_row + s2_ref[h, i2, tok]
                        w = e1_row * e2_ref[h, i2, tok]
                        g = g + jnp.where(pair >= tau_ref[h:h + 1, tok], w, 0.0)
                    act = g * _gelu_exact(hu_ref[k % 2, rows, tok])
                    act_ref[k % 2, rows, tok] = act.astype(bf16)

    def combine_sub(k):
        acc_ref[...] += jnp.dot(vt_ref[:, expert_rows(k)], act_ref[k % 2], preferred_element_type=f32)

    project_sub(0)
    for k in range(n_sub):
        if k + 1 < n_sub:
            project_sub(k + 1)
        activate_sub(k)
        combine_sub(k)

    @pl.when(eb == pl.num_programs(1) - 1)
    def _():
        y_ref[...] = h_ref[...] + acc_ref[...].T


def _peer(hnt, h, wq_t, sk1, sk2, u_bf, vt_bf):
    n = h.shape[0]
    t = PEER_T
    assert n % t == 0
    return pl.pallas_call(
        _peer_kernel,
        out_shape=jax.ShapeDtypeStruct((n, D_MODEL), f32),
        grid=(n // t, N_EXPERTS // PEER_EB),
        in_specs=[pl.BlockSpec((D_MODEL, t), lambda j, e: (0, j)),
                  pl.BlockSpec((t, D_MODEL), lambda j, e: (j, 0)),
                  _const_spec((2 * PEER_HEADS * PEER_HALF, D_MODEL)),
                  _const_spec((N_SUBKEYS, PEER_HALF)), _const_spec((N_SUBKEYS, PEER_HALF)),
                  pl.BlockSpec((PEER_EB, D_MODEL), lambda j, e: (e, 0)),
                  pl.BlockSpec((D_MODEL, PEER_EB), lambda j, e: (0, e))],
        out_specs=pl.BlockSpec((t, D_MODEL), lambda j, e: (j, 0)),
        scratch_shapes=[pltpu.VMEM((PEER_HEADS, N_SUBKEYS, t), f32)] * 4 + [
            pltpu.VMEM((PEER_HEADS, t), f32),
            pltpu.VMEM((PEER_TOPK, PEER_HEADS, t), f32),
            pltpu.VMEM((PEER_TOPK, PEER_HEADS, t), f32),
            pltpu.VMEM((2, PEER_SUB, t), f32),
            pltpu.VMEM((2, PEER_SUB, t), bf16),
            pltpu.VMEM((D_MODEL, t), f32)],
        compiler_params=_params(("parallel", "arbitrary")),
        name="peer",
    )(hnt, h, wq_t, sk1, sk2, u_bf, vt_bf)


def kernel(x_prompt, x_sample, cache_a1_kv, cache_a2_kv, cache_a3_kv, cache_b_kv, norm1, w_in, q_norm_a, k_norm_a,
           q_norm_b, k_norm_b, rel_bias_table, sinks_b, w_up_a, w_up_b, w_out, norm2, w_peer_q, sub_keys_1,
           sub_keys_2, expert_u, expert_v):
    batch, seq, _ = x_prompt.shape
    dec_batch, dec_seq, _ = x_sample.shape
    assert batch == 1 and dec_seq == DEC_SEQ

    w_in_bf = w_in.astype(bf16)
    wa_bf, wb_bf, wo_bf = w_up_a.astype(bf16), w_up_b.astype(bf16), w_out.astype(bf16)
    wq_t = w_peer_q.T.astype(bf16)
    sk1, sk2 = sub_keys_1.astype(bf16), sub_keys_2.astype(bf16)
    u_bf = expert_u.astype(bf16)
    vt_bf = expert_v.T.astype(bf16)
    ones_v = jnp.ones((A_HEADS * HEAD_DIM,), f32)
    gain = jnp.concatenate(
        [jnp.concatenate([jnp.tile(q_norm_a[g], A_HEADS), jnp.tile(k_norm_a[g], A_HEADS), ones_v])
         for g in range(N_A_GROUPS)]
        + [jnp.tile(q_norm_b, B_Q_HEADS), jnp.tile(k_norm_b, B_KV_HEADS), jnp.ones((B_KV_COLS,), f32)]
    ).astype(f32).reshape(1, QKV_COLS)
    lane_head = jnp.arange(MXU_COLS) // HEAD_DIM
    seg = (lane_head[:, None] == lane_head[None, :]).astype(bf16)
    b_heads = slice(N_A_GROUPS * A_HEADS, N_A_GROUPS * A_HEADS + B_Q_HEADS)
    tables_a = [rel_bias_table[:, g * A_HEADS:(g + 1) * A_HEADS] for g in range(N_A_GROUPS)]
    table_b = rel_bias_table[:, b_heads]
    sinks = sinks_b.astype(f32)

    xp = x_prompt.reshape(seq, D_MODEL)
    qkv_p, gates_p = _project(xp, norm1, w_in_bf, gain, seg)
    outs_p, lses_p = [], []
    for g, d in enumerate(A_DILATIONS):
        o, l = _band_attend_a(qkv_p, g, d, _band_bias(tables_a[g], d, A_SPAN))
        outs_p.append(o)
        lses_p.append(l)
    out_b_p = _band_attend_b(qkv_p, _band_bias(table_b, 1, B_WINDOW), sinks)
    h_p, hnt_p = _finish(xp, outs_p, lses_p, out_b_p, gates_p, wa_bf, wb_bf, wo_bf, norm2)
    y_prompt = _peer(hnt_p, h_p, wq_t, sk1, sk2, u_bf, vt_bf).reshape(batch, seq, D_MODEL)

    new_p = []
    for g, d in enumerate(A_DILATIONS):
        lp = min(A_SPAN * d, seq)
        c0 = g * A_GROUP_COLS + A_HEADS * HEAD_DIM
        new_p.append(qkv_p[seq - lp:, c0:c0 + 2 * A_HEADS * HEAD_DIM].reshape(batch, lp, 2, A_HEADS, HEAD_DIM))
    lpb = min(B_WINDOW, seq)
    new_b_prompt = qkv_p[seq - lpb:, A_COLS + B_Q_COLS:].reshape(batch, lpb, 2, B_KV_HEADS, HEAD_DIM)

    xs = x_sample.reshape(dec_batch * dec_seq, D_MODEL)
    qkv_s, gates_s = _project(xs, norm1, w_in_bf, gain, seg)
    aw = A_HEADS * HEAD_DIM
    outs_s, lses_s, new_s = [], [], []
    cfg_a = tuple((0, h * HEAD_DIM, h * HEAD_DIM) for h in range(A_HEADS))
    for g, (d, buf) in enumerate(zip(A_DILATIONS, (cache_a1_kv, cache_a2_kv, cache_a3_kv))):
        buf_len = buf.shape[1]
        bo, bn = _sample_bias(tables_a[g], d, A_SPAN, buf_len)
        o, l, nb = _sample_attend(qkv_s, buf.reshape(dec_batch, buf_len, 2 * aw),
                                  [(aw, 3 * g)], (aw, 3 * g + 1), (aw, 3 * g + 2), aw, cfg_a, bo, bn,
                                  None, True, f"sample_a{g}")
        outs_s.append(o)
        lses_s.append(l)
        new_s.append(nb.reshape(buf.shape))
    buf_len = cache_b_kv.shape[1]
    bo, bn = _sample_bias(table_b, 1, B_WINDOW, buf_len)
    cfg_b = tuple((h // 4, (h % 4) * HEAD_DIM, (h // 4) * HEAD_DIM) for h in range(B_Q_HEADS))
    qw = B_Q_COLS // 2
    k_blk = (A_COLS + B_Q_COLS) // B_KV_COLS
    out_b_s, new_b = _sample_attend(qkv_s, cache_b_kv.reshape(dec_batch, buf_len, 2 * B_KV_COLS),
                                    [(qw, A_COLS // qw), (qw, A_COLS // qw + 1)], (B_KV_COLS, k_blk),
                                    (B_KV_COLS, k_blk + 1), B_KV_COLS, cfg_b, bo, bn, sinks, False, "sample_b")
    new_b_sample = new_b.reshape(cache_b_kv.shape)
    h_s, hnt_s = _finish(xs, outs_s, lses_s, out_b_s, gates_s, wa_bf, wb_bf, wo_bf, norm2)
    y_sample = _peer(hnt_s, h_s, wq_t, sk1, sk2, u_bf, vt_bf).reshape(dec_batch, dec_seq, D_MODEL)

    return (y_prompt, y_sample, new_p[0], new_p[1], new_p[2], new_b_prompt,
            new_s[0], new_s[1], new_s[2], new_b_sample)
```

```python
import functools
import math

import jax
import jax.numpy as jnp
from jax import lax
from jax.experimental import pallas as pl
from jax.experimental.pallas import tpu as pltpu

f32 = jnp.float32
bf16 = jnp.bfloat16

D_MODEL = 1024
HEAD_DIM = 64
A_SPAN = 128
A_DILATIONS = (1, 4, 16)
A_HEADS = 4
N_A_GROUPS = 3
B_WINDOW = 128
B_Q_HEADS = 8
B_KV_HEADS = 2
N_BUCKETS = 32
MAX_DISTANCE = 2048
BLK = 128
DEC_SEQ = 8
A_GROUP_COLS = 3 * A_HEADS * HEAD_DIM
A_COLS = N_A_GROUPS * A_GROUP_COLS
B_Q_COLS = B_Q_HEADS * HEAD_DIM
B_KV_COLS = B_KV_HEADS * HEAD_DIM
QKV_COLS = A_COLS + B_Q_COLS + 2 * B_KV_COLS
GATE_COLS = 2 * D_MODEL
N_SUBKEYS = 128
N_EXPERTS = N_SUBKEYS * N_SUBKEYS
PEER_HEADS = 8
PEER_TOPK = 16
PEER_HALF = 128
EPS = 1e-6
NEG = -1e30
SCALE = HEAD_DIM ** -0.5

LANES = 128
SUBLANES = 8
MXU_COLS = 256
VMEM_LIMIT = 56 * 1024 * 1024

PROJ_TM = 256
PEER_T = 512
PEER_EB = 2048
PEER_SUB = 512
PEER_ROWS = 64
assert (PEER_EB // N_SUBKEYS) % SUBLANES == 0 and PEER_EB % PEER_SUB == 0 and PEER_SUB % N_SUBKEYS == 0


def _const_spec(shape):
    nd = len(shape)
    return pl.BlockSpec(shape, lambda *_: (0,) * nd, pipeline_mode=pl.Buffered(1))


def _params(sem):
    return pltpu.CompilerParams(dimension_semantics=sem, vmem_limit_bytes=VMEM_LIMIT)


_NORM_ALL, _NORM_NONE, _NORM_FIRST_HALF = 0, 1, 2


def _chunk_norm_mode(c):
    col = c * MXU_COLS
    if col < A_COLS:
        return _NORM_NONE if (col % A_GROUP_COLS) == 2 * A_HEADS * HEAD_DIM else _NORM_ALL
    if col < A_COLS + B_Q_COLS:
        return _NORM_ALL
    return _NORM_FIRST_HALF


def _proj_kernel(x_ref, g1_ref, w_ref, gain_ref, seg_ref, qkv_ref, gate_ref):
    x = x_ref[...]
    ms = jnp.mean(x * x, axis=-1, keepdims=True)
    xn = ((x * lax.rsqrt(ms + EPS)) * g1_ref[...]).astype(bf16)
    seg = seg_ref[...]
    for c in range(QKV_COLS // MXU_COLS):
        cols = slice(c * MXU_COLS, (c + 1) * MXU_COLS)
        hb = jnp.dot(xn, w_ref[:, cols], preferred_element_type=f32)
        mode = _chunk_norm_mode(c)
        if mode != _NORM_NONE:
            h2 = hb * hb
            hi = h2.astype(bf16)
            lo = (h2 - hi.astype(f32)).astype(bf16)
            ss = (jnp.dot(hi, seg, preferred_element_type=f32)
                  + jnp.dot(lo, seg, preferred_element_type=f32))
            normed = (hb * lax.rsqrt(ss * (1.0 / HEAD_DIM) + EPS)) * gain_ref[:, cols]
            if mode == _NORM_ALL:
                hb = normed
            else:
                hb = jnp.concatenate([normed[:, :LANES], hb[:, LANES:]], axis=1)
        qkv_ref[:, cols] = hb
    hg = jnp.dot(xn, w_ref[:, QKV_COLS:], preferred_element_type=f32)
    gate_ref[...] = jax.nn.sigmoid(hg)


def _project(x, norm1, w_in_bf, gain, seg):
    n = x.shape[0]
    tm = PROJ_TM
    assert n % tm == 0
    return pl.pallas_call(
        _proj_kernel,
        out_shape=(jax.ShapeDtypeStruct((n, QKV_COLS), f32), jax.ShapeDtypeStruct((n, GATE_COLS), f32)),
        grid=(n // tm,),
        in_specs=[
            pl.BlockSpec((tm, D_MODEL), lambda i: (i, 0)),
            _const_spec((1, D_MODEL)),
            _const_spec((D_MODEL, QKV_COLS + GATE_COLS)),
            _const_spec((1, QKV_COLS)),
            _const_spec((MXU_COLS, MXU_COLS)),
        ],
        out_specs=(pl.BlockSpec((tm, QKV_COLS), lambda i: (i, 0)),
                   pl.BlockSpec((tm, GATE_COLS), lambda i: (i, 0))),
        compiler_params=_params(("parallel",)),
        name="project",
    )(x, norm1.reshape(1, D_MODEL), w_in_bf, gain, seg)


def _t5_bucket(dist):
    exact = N_BUCKETS // 2
    d32 = jnp.maximum(dist, 1).astype(f32)
    large = exact + (jnp.log(d32 / exact) / math.log(MAX_DISTANCE / exact) * (N_BUCKETS - exact)).astype(jnp.int32)
    large = jnp.minimum(large, N_BUCKETS - 1)
    return jnp.where(dist < exact, dist, large)


def _toeplitz(period_vals, rows, width):
    reps = -(-rows * width // (width + 1))
    return jnp.tile(period_vals, (1, reps))[:, :rows * width].reshape(-1, rows, width)


def _band_bias(table_h, d, span):
    k = jnp.arange(2 * BLK + 1)
    steps = BLK - k
    vals = table_h[_t5_bucket(jnp.maximum(steps, 0) * d)].astype(f32).T
    vals = jnp.where(((steps >= 0) & (steps <= span))[None], vals, NEG)
    return _toeplitz(vals, BLK, 2 * BLK)


def _sample_bias(table_h, d, steps, buf_len):
    assert buf_len == steps * d
    width = buf_len + DEC_SEQ
    dist = buf_len - jnp.arange(width + 1)
    ok = (dist >= 0) & (dist % d == 0)
    vals = table_h[_t5_bucket(jnp.maximum(dist, 0))].astype(f32).T
    vals = jnp.where(ok[None], vals, NEG)
    b = _toeplitz(vals, DEC_SEQ, width).reshape(-1, width)
    old = b[:, :buf_len]
    new = jnp.pad(b[:, buf_len:], ((0, 0), (0, LANES - DEC_SEQ)), constant_values=NEG)
    return old, new


def _band_kernel(*refs, n_q, n_kv, head_cfg, has_sink, want_lse, dilation):
    q_refs = refs[:n_q]
    pos = n_q
    kp_refs, kc_refs, vp_refs, vc_refs = (refs[pos + i * n_kv:pos + (i + 1) * n_kv] for i in range(4))
    pos += 4 * n_kv
    bias_ref = refs[pos]
    pos += 1
    sink_ref = None
    if has_sink:
        sink_ref = refs[pos]
        pos += 1
    o_ref = refs[pos]
    pos += 1
    l_ref = None
    if want_lse:
        l_ref = refs[pos]
        pos += 1
    o_stage = refs[pos:pos + n_q] if dilation > 1 else None
    l_stage = refs[pos + n_q:pos + 2 * n_q] if dilation > 1 and want_lse else None

    blk = pl.program_id(0)
    col = lax.broadcasted_iota(jnp.int32, (BLK, 2 * BLK), 1)
    no_prev = jnp.where((col < BLK) & (blk == 0), NEG, 0.0).astype(f32)
    contract_last = (((1,), (1,)), ((), ()))
    heads_per_ref = LANES // HEAD_DIM

    def residue(r, carry):
        rows = pl.ds(r, BLK, stride=dilation) if dilation > 1 else slice(None)
        qs = [ref[rows, :] for ref in q_refs]
        kps, kcs = [ref[rows, :] for ref in kp_refs], [ref[rows, :] for ref in kc_refs]
        vps, vcs = [ref[rows, :] for ref in vp_refs], [ref[rows, :] for ref in vc_refs]
        outs, lses = [], []
        for h, (qi, qoff, ki, koff) in enumerate(head_cfg):
            q = qs[qi][:, qoff:qoff + HEAD_DIM].astype(bf16)
            k = jnp.concatenate([kps[ki][:, koff:koff + HEAD_DIM], kcs[ki][:, koff:koff + HEAD_DIM]],
                                axis=0).astype(bf16)
            v = jnp.concatenate([vps[ki][:, koff:koff + HEAD_DIM], vcs[ki][:, koff:koff + HEAD_DIM]],
                                axis=0).astype(bf16)
            s = lax.dot_general(q, k, contract_last, preferred_element_type=f32) * SCALE + bias_ref[h] + no_prev
            m = jnp.max(s, axis=-1, keepdims=True)
            if has_sink:
                sink = sink_ref[h]
                m = jnp.maximum(m, sink)
            e = jnp.exp(s - m)
            den = jnp.sum(e, axis=-1, keepdims=True)
            if has_sink:
                den = den + jnp.exp(sink - m)
            p = (e / den).astype(bf16)
            outs.append(jnp.dot(p, v, preferred_element_type=f32))
            if want_lse:
                lses.append(jnp.broadcast_to(m + jnp.log(den), (BLK, HEAD_DIM)))
        for j in range(n_q):
            o = jnp.concatenate(outs[j * heads_per_ref:(j + 1) * heads_per_ref], axis=1)
            if dilation > 1:
                o_stage[j][rows, :] = o
            else:
                o_ref[:, j * LANES:(j + 1) * LANES] = o
            if want_lse:
                lse = jnp.concatenate(lses[j * heads_per_ref:(j + 1) * heads_per_ref], axis=1)
                if dilation > 1:
                    l_stage[j][rows, :] = lse
                else:
                    l_ref[:, j * LANES:(j + 1) * LANES] = lse
        return carry

    if dilation > 1:
        lax.fori_loop(0, dilation, residue, 0)
        for j in range(n_q):
            o_ref[:, j * LANES:(j + 1) * LANES] = o_stage[j][...]
            if want_lse:
                l_ref[:, j * LANES:(j + 1) * LANES] = l_stage[j][...]
    else:
        residue(0, 0)


def _band_attend_a(qkv, g, d, bias):
    s_len = qkv.shape[0]
    rows = d * BLK
    assert s_len % rows == 0
    width = A_HEADS * HEAD_DIM
    n_ref = width // LANES
    base = g * 3 * n_ref

    def specs(part, prev):
        if prev:
            return [pl.BlockSpec((rows, LANES), lambda b, c=base + part * n_ref + j: (jnp.maximum(b - 1, 0), c))
                    for j in range(n_ref)]
        return [pl.BlockSpec((rows, LANES), lambda b, c=base + part * n_ref + j: (b, c)) for j in range(n_ref)]

    heads_per_ref = LANES // HEAD_DIM
    head_cfg = tuple((h // heads_per_ref, (h % heads_per_ref) * HEAD_DIM) * 2 for h in range(A_HEADS))
    in_specs = specs(0, False) + specs(1, True) + specs(1, False) + specs(2, True) + specs(2, False)
    scratch = [pltpu.VMEM((rows, LANES), f32)] * (2 * n_ref) if d > 1 else []
    return pl.pallas_call(
        functools.partial(_band_kernel, n_q=n_ref, n_kv=n_ref, head_cfg=head_cfg, has_sink=False, want_lse=True,
                          dilation=d),
        out_shape=(jax.ShapeDtypeStruct((s_len, width), f32), jax.ShapeDtypeStruct((s_len, width), f32)),
        grid=(s_len // rows,),
        in_specs=in_specs + [_const_spec((A_HEADS, BLK, 2 * BLK))],
        out_specs=(pl.BlockSpec((rows, width), lambda b: (b, 0)),
                   pl.BlockSpec((rows, width), lambda b: (b, 0))),
        scratch_shapes=scratch,
        compiler_params=_params(("parallel",)),
        name=f"band_a{g}",
    )(*([qkv] * len(in_specs)), bias)


def _band_attend_b(qkv, bias, sinks):
    s_len = qkv.shape[0]
    assert s_len % BLK == 0
    assert B_KV_COLS == LANES
    n_q = B_Q_COLS // LANES
    q_blk = A_COLS // LANES
    k_blk = (A_COLS + B_Q_COLS) // LANES
    heads_per_ref = LANES // HEAD_DIM
    group = B_Q_HEADS // B_KV_HEADS
    head_cfg = tuple((h // heads_per_ref, (h % heads_per_ref) * HEAD_DIM, 0, (h // group) * HEAD_DIM)
                     for h in range(B_Q_HEADS))

    def kv_spec(off, prev):
        if prev:
            return pl.BlockSpec((BLK, LANES), lambda b: (jnp.maximum(b - 1, 0), k_blk + off))
        return pl.BlockSpec((BLK, LANES), lambda b: (b, k_blk + off))

    in_specs = [pl.BlockSpec((BLK, LANES), lambda b, c=q_blk + j: (b, c)) for j in range(n_q)]
    in_specs += [kv_spec(0, True), kv_spec(0, False), kv_spec(1, True), kv_spec(1, False)]
    return pl.pallas_call(
        functools.partial(_band_kernel, n_q=n_q, n_kv=1, head_cfg=head_cfg, has_sink=True, want_lse=False,
                          dilation=1),
        out_shape=jax.ShapeDtypeStruct((s_len, B_Q_COLS), f32),
        grid=(s_len // BLK,),
        in_specs=in_specs + [_const_spec((B_Q_HEADS, BLK, 2 * BLK)), pl.BlockSpec(memory_space=pltpu.SMEM)],
        out_specs=pl.BlockSpec((BLK, B_Q_COLS), lambda b: (b, 0)),
        compiler_params=_params(("parallel",)),
        name="band_b",
    )(*([qkv] * len(in_specs)), bias, sinks)


def _sample_kernel(*refs, n_q, head_cfg, has_sink, want_lse, kv_cols):
    q_refs = refs[:n_q]
    kn_ref, vn_ref, buf_ref, bo_ref, bn_ref = refs[n_q:n_q + 5]
    pos = n_q + 5
    sink_ref = None
    if has_sink:
        sink_ref = refs[pos]
        pos += 1
    o_ref = refs[pos]
    pos += 1
    l_ref = None
    if want_lse:
        l_ref = refs[pos]
        pos += 1
    nb_ref = refs[pos]

    buf_len = buf_ref.shape[2]
    t_new = DEC_SEQ
    k_new = kn_ref[...]
    v_new = vn_ref[...]
    pad = jnp.zeros((LANES - t_new, kv_cols), f32)
    k_pad = jnp.concatenate([k_new, pad], axis=0)
    v_pad = jnp.concatenate([v_new, pad], axis=0)
    buf = buf_ref[0]

    new_t = jnp.concatenate([k_pad, v_pad], axis=1).T
    shifted = pltpu.roll(buf, buf_len - t_new, 1)
    lane = lax.broadcasted_iota(jnp.int32, (2 * kv_cols, LANES), 1)
    tail = jnp.where(lane >= LANES - t_new, pltpu.roll(new_t, LANES - t_new, 1), shifted[:, buf_len - LANES:])
    if buf_len > LANES:
        nb_ref[0, :, 0:buf_len - LANES] = shifted[:, 0:buf_len - LANES]
    nb_ref[0, :, buf_len - LANES:buf_len] = tail

    rows = []
    for (qi, qoff, kvoff) in head_cfg:
        q = q_refs[qi][:, qoff:qoff + HEAD_DIM]
        pieces = []
        if kvoff > 0:
            pieces.append(jnp.zeros((t_new, kvoff), f32))
        pieces.append(q)
        if kv_cols - kvoff - HEAD_DIM > 0:
            pieces.append(jnp.zeros((t_new, kv_cols - kvoff - HEAD_DIM), f32))
        rows.append(jnp.concatenate(pieces, axis=1) if len(pieces) > 1 else q)
    qrows = jnp.concatenate(rows, axis=0).astype(bf16)

    kt_old = buf[0:kv_cols, :].astype(bf16)
    vt_old = buf[kv_cols:2 * kv_cols, :].astype(bf16)
    k_pad = k_pad.astype(bf16)
    v_pad = v_pad.astype(bf16)

    contract_last = (((1,), (1,)), ((), ()))
    s_old = jnp.dot(qrows, kt_old, preferred_element_type=f32) * SCALE + bo_ref[...]
    s_new = lax.dot_general(qrows, k_pad, contract_last, preferred_element_type=f32) * SCALE + bn_ref[...]
    m = jnp.maximum(jnp.max(s_old, axis=-1, keepdims=True), jnp.max(s_new, axis=-1, keepdims=True))
    n_heads = len(head_cfg)
    sink_col = None
    if has_sink:
        sink_col = jnp.concatenate(
            [jnp.full((t_new, 1), sink_ref[h], f32) for h in range(n_heads)], axis=0)
        m = jnp.maximum(m, sink_col)
    e_old = jnp.exp(s_old - m)
    e_new = jnp.exp(s_new - m)
    den = jnp.sum(e_old, axis=-1, keepdims=True) + jnp.sum(e_new, axis=-1, keepdims=True)
    if has_sink:
        den = den + jnp.exp(sink_col - m)
    acc = (lax.dot_general((e_old / den).astype(bf16), vt_old, contract_last, preferred_element_type=f32)
           + jnp.dot((e_new / den).astype(bf16), v_pad, preferred_element_type=f32))
    lse = m + jnp.log(den)
    for h, (qi, qoff, kvoff) in enumerate(head_cfg):
        o_ref[:, h * HEAD_DIM:(h + 1) * HEAD_DIM] = acc[h * t_new:(h + 1) * t_new, kvoff:kvoff + HEAD_DIM]
        if want_lse:
            l_ref[:, h * HEAD_DIM:(h + 1) * HEAD_DIM] = jnp.broadcast_to(
                lse[h * t_new:(h + 1) * t_new, :], (t_new, HEAD_DIM))


def _position_minor(buf):
    bd, buf_len = buf.shape[:2]
    return jnp.transpose(buf, (0, 2, 3, 4, 1)).reshape(bd, -1, buf_len)


def _position_major(buf_t, shape):
    bd, buf_len, two, groups, hd = shape
    return jnp.transpose(buf_t.reshape(bd, two, groups, hd, buf_len), (0, 4, 1, 2, 3))


def _sample_attend(qkv, buf, q_blocks, k_block, v_block, kv_cols, head_cfg, bias_old, bias_new, sinks, want_lse, name):
    bd, width2, buf_len = buf.shape
    assert width2 == 2 * kv_cols and qkv.shape[0] == bd * DEC_SEQ and buf_len % LANES == 0
    n_heads = len(head_cfg)
    rows = n_heads * DEC_SEQ
    has_sink = sinks is not None

    def col_spec(width, idx):
        return pl.BlockSpec((DEC_SEQ, width), lambda b: (b, idx))

    in_specs = [col_spec(w, i) for (w, i) in q_blocks]
    in_specs += [col_spec(*k_block), col_spec(*v_block),
                 pl.BlockSpec((1, width2, buf_len), lambda b: (b, 0, 0)),
                 _const_spec((rows, buf_len)), _const_spec((rows, LANES))]
    args = [qkv] * (len(q_blocks) + 2) + [buf, bias_old, bias_new]
    if has_sink:
        in_specs.append(pl.BlockSpec(memory_space=pltpu.SMEM))
        args.append(sinks)
    out_w = n_heads * HEAD_DIM
    out_shape = [jax.ShapeDtypeStruct((bd * DEC_SEQ, out_w), f32)]
    out_specs = [pl.BlockSpec((DEC_SEQ, out_w), lambda b: (b, 0))]
    if want_lse:
        out_shape.append(jax.ShapeDtypeStruct((bd * DEC_SEQ, out_w), f32))
        out_specs.append(pl.BlockSpec((DEC_SEQ, out_w), lambda b: (b, 0)))
    out_shape.append(jax.ShapeDtypeStruct(buf.shape, f32))
    out_specs.append(pl.BlockSpec((1, width2, buf_len), lambda b: (b, 0, 0)))
    return pl.pallas_call(
        functools.partial(_sample_kernel, n_q=len(q_blocks), head_cfg=head_cfg, has_sink=has_sink,
                          want_lse=want_lse, kv_cols=kv_cols),
        out_shape=tuple(out_shape),
        grid=(bd,),
        in_specs=in_specs,
        out_specs=tuple(out_specs),
        compiler_params=_params(("parallel",)),
        name=name,
    )(*args)


def _finish_kernel(x_ref, oa0_ref, oa1_ref, oa2_ref, la0_ref, la1_ref, la2_ref, ob_ref, gate_ref,
                   wa_ref, wb_ref, wo_ref, g2_ref, h_ref, hnt_ref):
    l0, l1, l2 = la0_ref[...], la1_ref[...], la2_ref[...]
    m = jnp.maximum(jnp.maximum(l0, l1), l2)
    e0, e1, e2 = jnp.exp(l0 - m), jnp.exp(l1 - m), jnp.exp(l2 - m)
    den = e0 + e1 + e2
    oa = (e0 / den) * oa0_ref[...] + (e1 / den) * oa1_ref[...] + (e2 / den) * oa2_ref[...]
    ya = jnp.dot(oa.astype(bf16), wa_ref[...], preferred_element_type=f32)
    yb = jnp.dot(ob_ref[...].astype(bf16), wb_ref[...], preferred_element_type=f32)
    merged = gate_ref[:, :D_MODEL] * ya + gate_ref[:, D_MODEL:] * yb
    h = x_ref[...] + jnp.dot(merged.astype(bf16), wo_ref[...], preferred_element_type=f32)
    h_ref[...] = h
    ms = jnp.mean(h * h, axis=-1, keepdims=True)
    hn = (h * lax.rsqrt(ms + EPS)) * g2_ref[...]
    hnt_ref[...] = hn.T.astype(bf16)


def _finish(x, outs_a, lses_a, out_b, gates, wa_bf, wb_bf, wo_bf, norm2):
    n = x.shape[0]
    tm = PROJ_TM
    assert n % tm == 0
    aw = A_HEADS * HEAD_DIM

    def row_spec(w):
        return pl.BlockSpec((tm, w), lambda i: (i, 0))

    return pl.pallas_call(
        _finish_kernel,
        out_shape=(jax.ShapeDtypeStruct((n, D_MODEL), f32), jax.ShapeDtypeStruct((D_MODEL, n), bf16)),
        grid=(n // tm,),
        in_specs=[row_spec(D_MODEL)] + [row_spec(aw)] * 6 + [row_spec(B_Q_COLS), row_spec(GATE_COLS),
                  _const_spec((aw, D_MODEL)), _const_spec((B_Q_COLS, D_MODEL)), _const_spec((D_MODEL, D_MODEL)),
                  _const_spec((1, D_MODEL))],
        out_specs=(row_spec(D_MODEL), pl.BlockSpec((D_MODEL, tm), lambda i: (0, i))),
        compiler_params=_params(("parallel",)),
        name="finish",
    )(x, *outs_a, *lses_a, out_b, gates, wa_bf, wb_bf, wo_bf, norm2.reshape(1, D_MODEL))


def _batcher_pairs(n):
    pairs = []
    p = 1
    while p < n:
        k = p
        while k >= 1:
            for j in range(k % p, n - k, 2 * k):
                for i in range(min(k, n - j - k)):
                    if (i + j) // (2 * p) == (i + j + k) // (2 * p):
                        pairs.append((i + j, i + j + k))
            k //= 2
        p *= 2
    return pairs


_SORT16 = _batcher_pairs(PEER_TOPK)
_BITONIC16 = [(i, i + dist) for dist in (8, 4, 2, 1) for i in range(PEER_TOPK) if not i & dist]


def _exchange(xs, i, j):
    a, b = xs[i], xs[j]
    if b is None:
        return
    if a is None:
        xs[i], xs[j] = b, None
        return
    xs[i], xs[j] = jnp.maximum(a, b), jnp.minimum(a, b)


def _sort16_desc(xs):
    xs = list(xs)
    for i, j in _SORT16:
        _exchange(xs, i, j)
    return xs


def _merge_top16(xs, ys):
    zs = []
    for j in range(PEER_TOPK):
        a, b = xs[j], ys[PEER_TOPK - 1 - j]
        zs.append(b if a is None else a if b is None else jnp.maximum(a, b))
    for i, j in _BITONIC16:
        _exchange(zs, i, j)
    return zs


_CAND_PAIRS = [(a, b) for a in range(PEER_TOPK) for b in range(PEER_TOPK) if (a + 1) * (b + 1) <= PEER_TOPK]


def _peer_route(hnt_ref, wq_ref, sk1_ref, sk2_ref, s1_ref, s2_ref, e1_ref, e2_ref, tau_ref, top1_ref, top2_ref):
    hnt = hnt_ref[...]
    for half, (sk_ref, s_ref, top_ref) in enumerate(((sk1_ref, s1_ref, top1_ref), (sk2_ref, s2_ref, top2_ref))):
        for h in range(PEER_HEADS):
            r0 = (2 * h + half) * PEER_HALF
            qh = jnp.dot(wq_ref[r0:r0 + PEER_HALF, :], hnt, preferred_element_type=f32).astype(bf16)
            st = jnp.dot(sk_ref[...], qh, preferred_element_type=f32)
            s_ref[h] = st
            xs = _sort16_desc([st[j * SUBLANES:(j + 1) * SUBLANES, :] for j in range(N_SUBKEYS // SUBLANES)])
            for shift in (4, 2, 1):
                xs = _merge_top16(xs, [pltpu.roll(x, shift, 0) for x in xs])
            for a in range(PEER_TOPK):
                top_ref[a, h:h + 1, :] = xs[a][0:1, :]
    v1 = [top1_ref[a] for a in range(PEER_TOPK)]
    v2 = [top2_ref[a] for a in range(PEER_TOPK)]
    cands = [v1[a] + v2[b] for a, b in _CAND_PAIRS]
    cands += [None] * (-len(cands) % PEER_TOPK)
    groups = [_sort16_desc(cands[i::len(cands) // PEER_TOPK]) for i in range(len(cands) // PEER_TOPK)]
    while len(groups) > 1:
        groups = [_merge_top16(groups[i], groups[i + 1]) for i in range(0, len(groups), 2)]
    sc = groups[0]
    den = jnp.zeros_like(sc[0])
    for k in range(PEER_TOPK):
        den = den + jnp.exp(sc[k] - sc[0])
    inv_den = 1.0 / den
    tau_ref[...] = sc[PEER_TOPK - 1]
    for h in range(PEER_HEADS):
        e1_ref[h] = jnp.exp(s1_ref[h] - v1[0][h:h + 1, :]) * inv_den[h:h + 1, :]
        e2_ref[h] = jnp.exp(s2_ref[h] - v2[0][h:h + 1, :])


def _gelu_exact(x):
    return 0.5 * x * (1.0 + lax.erf(x * (2.0 ** -0.5)))


def _peer_kernel(hnt_ref, h_ref, wq_ref, sk1_ref, sk2_ref, u_ref, vt_ref, y_ref,
                 s1_ref, s2_ref, e1_ref, e2_ref, tau_ref, top1_ref, top2_ref, hu_ref, act_ref, acc_ref):
    eb = pl.program_id(1)
    n_sub = PEER_EB // PEER_SUB
    blocks = PEER_SUB // N_SUBKEYS

    @pl.when(eb == 0)
    def _():
        _peer_route(hnt_ref, wq_ref, sk1_ref, sk2_ref, s1_ref, s2_ref, e1_ref, e2_ref, tau_ref, top1_ref, top2_ref)
        acc_ref[...] = jnp.zeros_like(acc_ref)

    def expert_rows(k):
        return slice(k * PEER_SUB, (k + 1) * PEER_SUB)

    def project_sub(k):
        hu_ref[k % 2] = jnp.dot(u_ref[expert_rows(k), :], hnt_ref[...], preferred_element_type=f32)

    def activate_sub(k):
        for i in range(blocks):
            local = k * blocks + i
            group = pl.ds(pl.multiple_of(eb * (PEER_EB // N_SUBKEYS) + (local // SUBLANES) * SUBLANES, SUBLANES),
                          SUBLANES)
            sub = local % SUBLANES
            for c in range(PEER_T // LANES):
                tok = slice(c * LANES, (c + 1) * LANES)
                for part in range(N_SUBKEYS // PEER_ROWS):
                    i2 = slice(part * PEER_ROWS, (part + 1) * PEER_ROWS)
                    rows = slice(i * N_SUBKEYS + part * PEER_ROWS, i * N_SUBKEYS + (part + 1) * PEER_ROWS)
                    g = jnp.zeros((PEER_ROWS, LANES), f32)
                    for h in range(PEER_HEADS):
                        s1_row = s1_ref[h, group, tok][sub:sub + 1, :]
                        e1_row = e1_ref[h, group, tok][sub:sub + 1, :]
                        pair = s1_row + s2_ref[h, i2, tok]
                        w = e1_row * e2_ref[h, i2, tok]
                        g = g + jnp.where(pair >= tau_ref[h:h + 1, tok], w, 0.0)
                    act = g * _gelu_exact(hu_ref[k % 2, rows, tok])
                    act_ref[k % 2, rows, tok] = act.astype(bf16)

    def combine_sub(k):
        acc_ref[...] += jnp.dot(vt_ref[:, expert_rows(k)], act_ref[k % 2], preferred_element_type=f32)

    project_sub(0)
    for k in range(n_sub):
        if k + 1 < n_sub:
            project_sub(k + 1)
        activate_sub(k)
        combine_sub(k)

    @pl.when(eb == pl.num_programs(1) - 1)
    def _():
        y_ref[...] = h_ref[...] + acc_ref[...].T


def _peer(hnt, h, wq_t, sk1, sk2, u_bf, vt_bf):
    n = h.shape[0]
    t = PEER_T
    assert n % t == 0
    return pl.pallas_call(
        _peer_kernel,
        out_shape=jax.ShapeDtypeStruct((n, D_MODEL), f32),
        grid=(n // t, N_EXPERTS // PEER_EB),
        in_specs=[pl.BlockSpec((D_MODEL, t), lambda j, e: (0, j)),
                  pl.BlockSpec((t, D_MODEL), lambda j, e: (j, 0)),
                  _const_spec((2 * PEER_HEADS * PEER_HALF, D_MODEL)),
                  _const_spec((N_SUBKEYS, PEER_HALF)), _const_spec((N_SUBKEYS, PEER_HALF)),
                  pl.BlockSpec((PEER_EB, D_MODEL), lambda j, e: (e, 0)),
                  pl.BlockSpec((D_MODEL, PEER_EB), lambda j, e: (0, e))],
        out_specs=pl.BlockSpec((t, D_MODEL), lambda j, e: (j, 0)),
        scratch_shapes=[pltpu.VMEM((PEER_HEADS, N_SUBKEYS, t), f32)] * 4 + [
            pltpu.VMEM((PEER_HEADS, t), f32),
            pltpu.VMEM((PEER_TOPK, PEER_HEADS, t), f32),
            pltpu.VMEM((PEER_TOPK, PEER_HEADS, t), f32),
            pltpu.VMEM((2, PEER_SUB, t), f32),
            pltpu.VMEM((2, PEER_SUB, t), bf16),
            pltpu.VMEM((D_MODEL, t), f32)],
        compiler_params=_params(("parallel", "arbitrary")),
        name="peer",
    )(hnt, h, wq_t, sk1, sk2, u_bf, vt_bf)


def kernel(x_prompt, x_sample, cache_a1_kv, cache_a2_kv, cache_a3_kv, cache_b_kv, norm1, w_in, q_norm_a, k_norm_a,
           q_norm_b, k_norm_b, rel_bias_table, sinks_b, w_up_a, w_up_b, w_out, norm2, w_peer_q, sub_keys_1,
           sub_keys_2, expert_u, expert_v):
    batch, seq, _ = x_prompt.shape
    dec_batch, dec_seq, _ = x_sample.shape
    assert batch == 1 and dec_seq == DEC_SEQ

    w_in_bf = w_in.astype(bf16)
    wa_bf, wb_bf, wo_bf = w_up_a.astype(bf16), w_up_b.astype(bf16), w_out.astype(bf16)
    wq_t = w_peer_q.T.astype(bf16)
    sk1, sk2 = sub_keys_1.astype(bf16), sub_keys_2.astype(bf16)
    u_bf = expert_u.astype(bf16)
    vt_bf = expert_v.T.astype(bf16)
    ones_v = jnp.ones((A_HEADS * HEAD_DIM,), f32)
    gain = jnp.concatenate(
        [jnp.concatenate([jnp.tile(q_norm_a[g], A_HEADS), jnp.tile(k_norm_a[g], A_HEADS), ones_v])
         for g in range(N_A_GROUPS)]
        + [jnp.tile(q_norm_b, B_Q_HEADS), jnp.tile(k_norm_b, B_KV_HEADS), jnp.ones((B_KV_COLS,), f32)]
    ).astype(f32).reshape(1, QKV_COLS)
    lane_head = jnp.arange(MXU_COLS) // HEAD_DIM
    seg = (lane_head[:, None] == lane_head[None, :]).astype(bf16)
    b_heads = slice(N_A_GROUPS * A_HEADS, N_A_GROUPS * A_HEADS + B_Q_HEADS)
    tables_a = [rel_bias_table[:, g * A_HEADS:(g + 1) * A_HEADS] for g in range(N_A_GROUPS)]
    table_b = rel_bias_table[:, b_heads]
    sinks = sinks_b.astype(f32)

    xp = x_prompt.reshape(seq, D_MODEL)
    qkv_p, gates_p = _project(xp, norm1, w_in_bf, gain, seg)
    outs_p, lses_p = [], []
    for g, d in enumerate(A_DILATIONS):
        o, l = _band_attend_a(qkv_p, g, d, _band_bias(tables_a[g], d, A_SPAN))
        outs_p.append(o)
        lses_p.append(l)
    out_b_p = _band_attend_b(qkv_p, _band_bias(table_b, 1, B_WINDOW), sinks)
    h_p, hnt_p = _finish(xp, outs_p, lses_p, out_b_p, gates_p, wa_bf, wb_bf, wo_bf, norm2)
    y_prompt = _peer(hnt_p, h_p, wq_t, sk1, sk2, u_bf, vt_bf).reshape(batch, seq, D_MODEL)

    new_p = []
    for g, d in enumerate(A_DILATIONS):
        lp = min(A_SPAN * d, seq)
        c0 = g * A_GROUP_COLS + A_HEADS * HEAD_DIM
        new_p.append(qkv_p[seq - lp:, c0:c0 + 2 * A_HEADS * HEAD_DIM].reshape(batch, lp, 2, A_HEADS, HEAD_DIM))
    lpb = min(B_WINDOW, seq)
    new_b_prompt = qkv_p[seq - lpb:, A_COLS + B_Q_COLS:].reshape(batch, lpb, 2, B_KV_HEADS, HEAD_DIM)

    xs = x_sample.reshape(dec_batch * dec_seq, D_MODEL)
    qkv_s, gates_s = _project(xs, norm1, w_in_bf, gain, seg)
    aw = A_HEADS * HEAD_DIM
    outs_s, lses_s, new_s = [], [], []
    cfg_a = tuple((0, h * HEAD_DIM, h * HEAD_DIM) for h in range(A_HEADS))
    for g, (d, buf) in enumerate(zip(A_DILATIONS, (cache_a1_kv, cache_a2_kv, cache_a3_kv))):
        buf_len = buf.shape[1]
        bo, bn = _sample_bias(tables_a[g], d, A_SPAN, buf_len)
        o, l, nb = _sample_attend(qkv_s, _position_minor(buf), [(aw, 3 * g)], (aw, 3 * g + 1), (aw, 3 * g + 2),
                                  aw, cfg_a, bo, bn, None, True, f"sample_a{g}")
        outs_s.append(o)
        lses_s.append(l)
        new_s.append(_position_major(nb, buf.shape))
    buf_len = cache_b_kv.shape[1]
    bo, bn = _sample_bias(table_b, 1, B_WINDOW, buf_len)
    cfg_b = tuple((h // 4, (h % 4) * HEAD_DIM, (h // 4) * HEAD_DIM) for h in range(B_Q_HEADS))
    qw = B_Q_COLS // 2
    k_blk = (A_COLS + B_Q_COLS) // B_KV_COLS
    out_b_s, new_b = _sample_attend(qkv_s, _position_minor(cache_b_kv),
                                    [(qw, A_COLS // qw), (qw, A_COLS // qw + 1)], (B_KV_COLS, k_blk),
                                    (B_KV_COLS, k_blk + 1), B_KV_COLS, cfg_b, bo, bn, sinks, False, "sample_b")
    new_b_sample = _position_major(new_b, cache_b_kv.shape)
    h_s, hnt_s = _finish(xs, outs_s, lses_s, out_b_s, gates_s, wa_bf, wb_bf, wo_bf, norm2)
    y_sample = _peer(hnt_s, h_s, wq_t, sk1, sk2, u_bf, vt_bf).reshape(dec_batch, dec_seq, D_MODEL)

    return (y_prompt, y_sample, new_p[0], new_p[1], new_p[2], new_b_prompt,
            new_s[0], new_s[1], new_s[2], new_b_sample)
```

```python
import functools
import math

import jax
import jax.numpy as jnp
from jax import lax
from jax.experimental import pallas as pl
from jax.experimental.pallas import tpu as pltpu

f32 = jnp.float32
bf16 = jnp.bfloat16

D_MODEL = 1024
HEAD_DIM = 64
A_SPAN = 128
A_DILATIONS = (1, 4, 16)
A_HEADS = 4
N_A_GROUPS = 3
B_WINDOW = 128
B_Q_HEADS = 8
B_KV_HEADS = 2
N_BUCKETS = 32
MAX_DISTANCE = 2048
BLK = 128
DEC_SEQ = 8
A_GROUP_COLS = 3 * A_HEADS * HEAD_DIM
A_COLS = N_A_GROUPS * A_GROUP_COLS
B_Q_COLS = B_Q_HEADS * HEAD_DIM
B_KV_COLS = B_KV_HEADS * HEAD_DIM
QKV_COLS = A_COLS + B_Q_COLS + 2 * B_KV_COLS
GATE_COLS = 2 * D_MODEL
N_SUBKEYS = 128
N_EXPERTS = N_SUBKEYS * N_SUBKEYS
PEER_HEADS = 8
PEER_TOPK = 16
PEER_HALF = 128
EPS = 1e-6
NEG = -1e30
SCALE = HEAD_DIM ** -0.5

LANES = 128
SUBLANES = 8
MXU_COLS = 256
VMEM_LIMIT = 56 * 1024 * 1024

PROJ_TM = 256
PEER_T = 512
PEER_EB = 2048
PEER_SUB = 512
PEER_ROWS = 128
assert (PEER_EB // N_SUBKEYS) % SUBLANES == 0 and PEER_EB % PEER_SUB == 0 and PEER_SUB % N_SUBKEYS == 0


def _const_spec(shape):
    nd = len(shape)
    return pl.BlockSpec(shape, lambda *_: (0,) * nd, pipeline_mode=pl.Buffered(1))


def _params(sem):
    return pltpu.CompilerParams(dimension_semantics=sem, vmem_limit_bytes=VMEM_LIMIT)


_NORM_ALL, _NORM_NONE, _NORM_FIRST_HALF = 0, 1, 2


def _chunk_norm_mode(c):
    col = c * MXU_COLS
    if col < A_COLS:
        return _NORM_NONE if (col % A_GROUP_COLS) == 2 * A_HEADS * HEAD_DIM else _NORM_ALL
    if col < A_COLS + B_Q_COLS:
        return _NORM_ALL
    return _NORM_FIRST_HALF


def _proj_kernel(x_ref, g1_ref, w_ref, gain_ref, seg_ref, qkv_ref, gate_ref):
    x = x_ref[...]
    ms = jnp.mean(x * x, axis=-1, keepdims=True)
    xn = ((x * lax.rsqrt(ms + EPS)) * g1_ref[...]).astype(bf16)
    seg = seg_ref[...]
    for c in range(QKV_COLS // MXU_COLS):
        cols = slice(c * MXU_COLS, (c + 1) * MXU_COLS)
        hb = jnp.dot(xn, w_ref[:, cols], preferred_element_type=f32)
        mode = _chunk_norm_mode(c)
        if mode != _NORM_NONE:
            h2 = hb * hb
            hi = h2.astype(bf16)
            lo = (h2 - hi.astype(f32)).astype(bf16)
            ss = (jnp.dot(hi, seg, preferred_element_type=f32)
                  + jnp.dot(lo, seg, preferred_element_type=f32))
            normed = (hb * lax.rsqrt(ss * (1.0 / HEAD_DIM) + EPS)) * gain_ref[:, cols]
            if mode == _NORM_ALL:
                hb = normed
            else:
                hb = jnp.concatenate([normed[:, :LANES], hb[:, LANES:]], axis=1)
        qkv_ref[:, cols] = hb
    hg = jnp.dot(xn, w_ref[:, QKV_COLS:], preferred_element_type=f32)
    gate_ref[...] = jax.nn.sigmoid(hg)


def _project(x, norm1, w_in_bf, gain, seg):
    n = x.shape[0]
    tm = PROJ_TM
    assert n % tm == 0
    return pl.pallas_call(
        _proj_kernel,
        out_shape=(jax.ShapeDtypeStruct((n, QKV_COLS), f32), jax.ShapeDtypeStruct((n, GATE_COLS), f32)),
        grid=(n // tm,),
        in_specs=[
            pl.BlockSpec((tm, D_MODEL), lambda i: (i, 0)),
            _const_spec((1, D_MODEL)),
            _const_spec((D_MODEL, QKV_COLS + GATE_COLS)),
            _const_spec((1, QKV_COLS)),
            _const_spec((MXU_COLS, MXU_COLS)),
        ],
        out_specs=(pl.BlockSpec((tm, QKV_COLS), lambda i: (i, 0)),
                   pl.BlockSpec((tm, GATE_COLS), lambda i: (i, 0))),
        compiler_params=_params(("parallel",)),
        name="project",
    )(x, norm1.reshape(1, D_MODEL), w_in_bf, gain, seg)


def _t5_bucket(dist):
    exact = N_BUCKETS // 2
    d32 = jnp.maximum(dist, 1).astype(f32)
    large = exact + (jnp.log(d32 / exact) / math.log(MAX_DISTANCE / exact) * (N_BUCKETS - exact)).astype(jnp.int32)
    large = jnp.minimum(large, N_BUCKETS - 1)
    return jnp.where(dist < exact, dist, large)


def _toeplitz(period_vals, rows, width):
    reps = -(-rows * width // (width + 1))
    return jnp.tile(period_vals, (1, reps))[:, :rows * width].reshape(-1, rows, width)


def _band_bias(table_h, d, span):
    k = jnp.arange(2 * BLK + 1)
    steps = BLK - k
    vals = table_h[_t5_bucket(jnp.maximum(steps, 0) * d)].astype(f32).T
    vals = jnp.where(((steps >= 0) & (steps <= span))[None], vals, NEG)
    return _toeplitz(vals, BLK, 2 * BLK)


def _sample_bias(table_h, d, steps, buf_len):
    assert buf_len == steps * d
    width = buf_len + DEC_SEQ
    dist = buf_len - jnp.arange(width + 1)
    ok = (dist >= 0) & (dist % d == 0)
    vals = table_h[_t5_bucket(jnp.maximum(dist, 0))].astype(f32).T
    vals = jnp.where(ok[None], vals, NEG)
    b = _toeplitz(vals, DEC_SEQ, width).reshape(-1, width)
    old = b[:, :buf_len]
    new = jnp.pad(b[:, buf_len:], ((0, 0), (0, LANES - DEC_SEQ)), constant_values=NEG)
    return old, new


def _band_kernel(*refs, n_q, n_kv, head_cfg, has_sink, want_lse, dilation):
    q_refs = refs[:n_q]
    pos = n_q
    kp_refs, kc_refs, vp_refs, vc_refs = (refs[pos + i * n_kv:pos + (i + 1) * n_kv] for i in range(4))
    pos += 4 * n_kv
    bias_ref = refs[pos]
    pos += 1
    sink_ref = None
    if has_sink:
        sink_ref = refs[pos]
        pos += 1
    o_ref = refs[pos]
    pos += 1
    l_ref = None
    if want_lse:
        l_ref = refs[pos]
        pos += 1
    o_stage = refs[pos:pos + n_q] if dilation > 1 else None
    l_stage = refs[pos + n_q:pos + 2 * n_q] if dilation > 1 and want_lse else None

    blk = pl.program_id(0)
    col = lax.broadcasted_iota(jnp.int32, (BLK, 2 * BLK), 1)
    no_prev = jnp.where((col < BLK) & (blk == 0), NEG, 0.0).astype(f32)
    contract_last = (((1,), (1,)), ((), ()))
    heads_per_ref = LANES // HEAD_DIM

    def residue(r, carry):
        rows = pl.ds(r, BLK, stride=dilation) if dilation > 1 else slice(None)
        qs = [ref[rows, :] for ref in q_refs]
        kps, kcs = [ref[rows, :] for ref in kp_refs], [ref[rows, :] for ref in kc_refs]
        vps, vcs = [ref[rows, :] for ref in vp_refs], [ref[rows, :] for ref in vc_refs]
        outs, lses = [], []
        for h, (qi, qoff, ki, koff) in enumerate(head_cfg):
            q = qs[qi][:, qoff:qoff + HEAD_DIM].astype(bf16)
            k = jnp.concatenate([kps[ki][:, koff:koff + HEAD_DIM], kcs[ki][:, koff:koff + HEAD_DIM]],
                                axis=0).astype(bf16)
            v = jnp.concatenate([vps[ki][:, koff:koff + HEAD_DIM], vcs[ki][:, koff:koff + HEAD_DIM]],
                                axis=0).astype(bf16)
            s = lax.dot_general(q, k, contract_last, preferred_element_type=f32) * SCALE + bias_ref[h] + no_prev
            m = jnp.max(s, axis=-1, keepdims=True)
            if has_sink:
                sink = sink_ref[h]
                m = jnp.maximum(m, sink)
            e = jnp.exp(s - m)
            den = jnp.sum(e, axis=-1, keepdims=True)
            if has_sink:
                den = den + jnp.exp(sink - m)
            p = (e / den).astype(bf16)
            outs.append(jnp.dot(p, v, preferred_element_type=f32))
            if want_lse:
                lses.append(jnp.broadcast_to(m + jnp.log(den), (BLK, HEAD_DIM)))
        for j in range(n_q):
            o = jnp.concatenate(outs[j * heads_per_ref:(j + 1) * heads_per_ref], axis=1)
            if dilation > 1:
                o_stage[j][rows, :] = o
            else:
                o_ref[:, j * LANES:(j + 1) * LANES] = o
            if want_lse:
                lse = jnp.concatenate(lses[j * heads_per_ref:(j + 1) * heads_per_ref], axis=1)
                if dilation > 1:
                    l_stage[j][rows, :] = lse
                else:
                    l_ref[:, j * LANES:(j + 1) * LANES] = lse
        return carry

    if dilation > 1:
        lax.fori_loop(0, dilation, residue, 0)
        for j in range(n_q):
            o_ref[:, j * LANES:(j + 1) * LANES] = o_stage[j][...]
            if want_lse:
                l_ref[:, j * LANES:(j + 1) * LANES] = l_stage[j][...]
    else:
        residue(0, 0)


def _band_attend_a(qkv, g, d, bias):
    s_len = qkv.shape[0]
    rows = d * BLK
    assert s_len % rows == 0
    width = A_HEADS * HEAD_DIM
    n_ref = width // LANES
    base = g * 3 * n_ref

    def specs(part, prev):
        if prev:
            return [pl.BlockSpec((rows, LANES), lambda b, c=base + part * n_ref + j: (jnp.maximum(b - 1, 0), c))
                    for j in range(n_ref)]
        return [pl.BlockSpec((rows, LANES), lambda b, c=base + part * n_ref + j: (b, c)) for j in range(n_ref)]

    heads_per_ref = LANES // HEAD_DIM
    head_cfg = tuple((h // heads_per_ref, (h % heads_per_ref) * HEAD_DIM) * 2 for h in range(A_HEADS))
    in_specs = specs(0, False) + specs(1, True) + specs(1, False) + specs(2, True) + specs(2, False)
    scratch = [pltpu.VMEM((rows, LANES), f32)] * (2 * n_ref) if d > 1 else []
    return pl.pallas_call(
        functools.partial(_band_kernel, n_q=n_ref, n_kv=n_ref, head_cfg=head_cfg, has_sink=False, want_lse=True,
                          dilation=d),
        out_shape=(jax.ShapeDtypeStruct((s_len, width), f32), jax.ShapeDtypeStruct((s_len, width), f32)),
        grid=(s_len // rows,),
        in_specs=in_specs + [_const_spec((A_HEADS, BLK, 2 * BLK))],
        out_specs=(pl.BlockSpec((rows, width), lambda b: (b, 0)),
                   pl.BlockSpec((rows, width), lambda b: (b, 0))),
        scratch_shapes=scratch,
        compiler_params=_params(("parallel",)),
        name=f"band_a{g}",
    )(*([qkv] * len(in_specs)), bias)


def _band_attend_b(qkv, bias, sinks):
    s_len = qkv.shape[0]
    assert s_len % BLK == 0
    assert B_KV_COLS == LANES
    n_q = B_Q_COLS // LANES
    q_blk = A_COLS // LANES
    k_blk = (A_COLS + B_Q_COLS) // LANES
    heads_per_ref = LANES // HEAD_DIM
    group = B_Q_HEADS // B_KV_HEADS
    head_cfg = tuple((h // heads_per_ref, (h % heads_per_ref) * HEAD_DIM, 0, (h // group) * HEAD_DIM)
                     for h in range(B_Q_HEADS))

    def kv_spec(off, prev):
        if prev:
            return pl.BlockSpec((BLK, LANES), lambda b: (jnp.maximum(b - 1, 0), k_blk + off))
        return pl.BlockSpec((BLK, LANES), lambda b: (b, k_blk + off))

    in_specs = [pl.BlockSpec((BLK, LANES), lambda b, c=q_blk + j: (b, c)) for j in range(n_q)]
    in_specs += [kv_spec(0, True), kv_spec(0, False), kv_spec(1, True), kv_spec(1, False)]
    return pl.pallas_call(
        functools.partial(_band_kernel, n_q=n_q, n_kv=1, head_cfg=head_cfg, has_sink=True, want_lse=False,
                          dilation=1),
        out_shape=jax.ShapeDtypeStruct((s_len, B_Q_COLS), f32),
        grid=(s_len // BLK,),
        in_specs=in_specs + [_const_spec((B_Q_HEADS, BLK, 2 * BLK)), pl.BlockSpec(memory_space=pltpu.SMEM)],
        out_specs=pl.BlockSpec((BLK, B_Q_COLS), lambda b: (b, 0)),
        compiler_params=_params(("parallel",)),
        name="band_b",
    )(*([qkv] * len(in_specs)), bias, sinks)


def _sample_kernel(*refs, n_q, head_cfg, has_sink, want_lse, kv_cols):
    q_refs = refs[:n_q]
    kn_ref, vn_ref, buf_ref, bo_ref, bn_ref = refs[n_q:n_q + 5]
    pos = n_q + 5
    sink_ref = None
    if has_sink:
        sink_ref = refs[pos]
        pos += 1
    o_ref = refs[pos]
    pos += 1
    l_ref = None
    if want_lse:
        l_ref = refs[pos]
        pos += 1
    nb_ref = refs[pos]

    buf_len = buf_ref.shape[2]
    t_new = DEC_SEQ
    k_new = kn_ref[...]
    v_new = vn_ref[...]
    pad = jnp.zeros((LANES - t_new, kv_cols), f32)
    k_pad = jnp.concatenate([k_new, pad], axis=0)
    v_pad = jnp.concatenate([v_new, pad], axis=0)
    buf = buf_ref[0]

    new_t = jnp.concatenate([k_pad, v_pad], axis=1).T
    shifted = pltpu.roll(buf, buf_len - t_new, 1)
    lane = lax.broadcasted_iota(jnp.int32, (2 * kv_cols, LANES), 1)
    tail = jnp.where(lane >= LANES - t_new, pltpu.roll(new_t, LANES - t_new, 1), shifted[:, buf_len - LANES:])
    if buf_len > LANES:
        nb_ref[0, :, 0:buf_len - LANES] = shifted[:, 0:buf_len - LANES]
    nb_ref[0, :, buf_len - LANES:buf_len] = tail

    rows = []
    for (qi, qoff, kvoff) in head_cfg:
        q = q_refs[qi][:, qoff:qoff + HEAD_DIM]
        pieces = []
        if kvoff > 0:
            pieces.append(jnp.zeros((t_new, kvoff), f32))
        pieces.append(q)
        if kv_cols - kvoff - HEAD_DIM > 0:
            pieces.append(jnp.zeros((t_new, kv_cols - kvoff - HEAD_DIM), f32))
        rows.append(jnp.concatenate(pieces, axis=1) if len(pieces) > 1 else q)
    qrows = jnp.concatenate(rows, axis=0).astype(bf16)

    kt_old = buf[0:kv_cols, :].astype(bf16)
    vt_old = buf[kv_cols:2 * kv_cols, :].astype(bf16)
    k_pad = k_pad.astype(bf16)
    v_pad = v_pad.astype(bf16)

    contract_last = (((1,), (1,)), ((), ()))
    s_old = jnp.dot(qrows, kt_old, preferred_element_type=f32) * SCALE + bo_ref[...]
    s_new = lax.dot_general(qrows, k_pad, contract_last, preferred_element_type=f32) * SCALE + bn_ref[...]
    m = jnp.maximum(jnp.max(s_old, axis=-1, keepdims=True), jnp.max(s_new, axis=-1, keepdims=True))
    n_heads = len(head_cfg)
    sink_col = None
    if has_sink:
        sink_col = jnp.concatenate(
            [jnp.full((t_new, 1), sink_ref[h], f32) for h in range(n_heads)], axis=0)
        m = jnp.maximum(m, sink_col)
    e_old = jnp.exp(s_old - m)
    e_new = jnp.exp(s_new - m)
    den = jnp.sum(e_old, axis=-1, keepdims=True) + jnp.sum(e_new, axis=-1, keepdims=True)
    if has_sink:
        den = den + jnp.exp(sink_col - m)
    acc = (lax.dot_general((e_old / den).astype(bf16), vt_old, contract_last, preferred_element_type=f32)
           + jnp.dot((e_new / den).astype(bf16), v_pad, preferred_element_type=f32))
    lse = m + jnp.log(den)
    for h, (qi, qoff, kvoff) in enumerate(head_cfg):
        o_ref[:, h * HEAD_DIM:(h + 1) * HEAD_DIM] = acc[h * t_new:(h + 1) * t_new, kvoff:kvoff + HEAD_DIM]
        if want_lse:
            l_ref[:, h * HEAD_DIM:(h + 1) * HEAD_DIM] = jnp.broadcast_to(
                lse[h * t_new:(h + 1) * t_new, :], (t_new, HEAD_DIM))


def _position_minor(buf):
    bd, buf_len = buf.shape[:2]
    return jnp.transpose(buf, (0, 2, 3, 4, 1)).reshape(bd, -1, buf_len)


def _position_major(buf_t, shape):
    bd, buf_len, two, groups, hd = shape
    return jnp.transpose(buf_t.reshape(bd, two, groups, hd, buf_len), (0, 4, 1, 2, 3))


def _sample_attend(qkv, buf, q_blocks, k_block, v_block, kv_cols, head_cfg, bias_old, bias_new, sinks, want_lse, name):
    bd, width2, buf_len = buf.shape
    assert width2 == 2 * kv_cols and qkv.shape[0] == bd * DEC_SEQ and buf_len % LANES == 0
    n_heads = len(head_cfg)
    rows = n_heads * DEC_SEQ
    has_sink = sinks is not None

    def col_spec(width, idx):
        return pl.BlockSpec((DEC_SEQ, width), lambda b: (b, idx))

    in_specs = [col_spec(w, i) for (w, i) in q_blocks]
    in_specs += [col_spec(*k_block), col_spec(*v_block),
                 pl.BlockSpec((1, width2, buf_len), lambda b: (b, 0, 0)),
                 _const_spec((rows, buf_len)), _const_spec((rows, LANES))]
    args = [qkv] * (len(q_blocks) + 2) + [buf, bias_old, bias_new]
    if has_sink:
        in_specs.append(pl.BlockSpec(memory_space=pltpu.SMEM))
        args.append(sinks)
    out_w = n_heads * HEAD_DIM
    out_shape = [jax.ShapeDtypeStruct((bd * DEC_SEQ, out_w), f32)]
    out_specs = [pl.BlockSpec((DEC_SEQ, out_w), lambda b: (b, 0))]
    if want_lse:
        out_shape.append(jax.ShapeDtypeStruct((bd * DEC_SEQ, out_w), f32))
        out_specs.append(pl.BlockSpec((DEC_SEQ, out_w), lambda b: (b, 0)))
    out_shape.append(jax.ShapeDtypeStruct(buf.shape, f32))
    out_specs.append(pl.BlockSpec((1, width2, buf_len), lambda b: (b, 0, 0)))
    return pl.pallas_call(
        functools.partial(_sample_kernel, n_q=len(q_blocks), head_cfg=head_cfg, has_sink=has_sink,
                          want_lse=want_lse, kv_cols=kv_cols),
        out_shape=tuple(out_shape),
        grid=(bd,),
        in_specs=in_specs,
        out_specs=tuple(out_specs),
        compiler_params=_params(("parallel",)),
        name=name,
    )(*args)


def _finish_kernel(x_ref, oa0_ref, oa1_ref, oa2_ref, la0_ref, la1_ref, la2_ref, ob_ref, gate_ref,
                   wa_ref, wb_ref, wo_ref, g2_ref, h_ref, hnt_ref):
    l0, l1, l2 = la0_ref[...], la1_ref[...], la2_ref[...]
    m = jnp.maximum(jnp.maximum(l0, l1), l2)
    e0, e1, e2 = jnp.exp(l0 - m), jnp.exp(l1 - m), jnp.exp(l2 - m)
    den = e0 + e1 + e2
    oa = (e0 / den) * oa0_ref[...] + (e1 / den) * oa1_ref[...] + (e2 / den) * oa2_ref[...]
    ya = jnp.dot(oa.astype(bf16), wa_ref[...], preferred_element_type=f32)
    yb = jnp.dot(ob_ref[...].astype(bf16), wb_ref[...], preferred_element_type=f32)
    merged = gate_ref[:, :D_MODEL] * ya + gate_ref[:, D_MODEL:] * yb
    h = x_ref[...] + jnp.dot(merged.astype(bf16), wo_ref[...], preferred_element_type=f32)
    h_ref[...] = h
    ms = jnp.mean(h * h, axis=-1, keepdims=True)
    hn = (h * lax.rsqrt(ms + EPS)) * g2_ref[...]
    hnt_ref[...] = hn.T.astype(bf16)


def _finish(x, outs_a, lses_a, out_b, gates, wa_bf, wb_bf, wo_bf, norm2):
    n = x.shape[0]
    tm = PROJ_TM
    assert n % tm == 0
    aw = A_HEADS * HEAD_DIM

    def row_spec(w):
        return pl.BlockSpec((tm, w), lambda i: (i, 0))

    return pl.pallas_call(
        _finish_kernel,
        out_shape=(jax.ShapeDtypeStruct((n, D_MODEL), f32), jax.ShapeDtypeStruct((D_MODEL, n), bf16)),
        grid=(n // tm,),
        in_specs=[row_spec(D_MODEL)] + [row_spec(aw)] * 6 + [row_spec(B_Q_COLS), row_spec(GATE_COLS),
                  _const_spec((aw, D_MODEL)), _const_spec((B_Q_COLS, D_MODEL)), _const_spec((D_MODEL, D_MODEL)),
                  _const_spec((1, D_MODEL))],
        out_specs=(row_spec(D_MODEL), pl.BlockSpec((D_MODEL, tm), lambda i: (0, i))),
        compiler_params=_params(("parallel",)),
        name="finish",
    )(x, *outs_a, *lses_a, out_b, gates, wa_bf, wb_bf, wo_bf, norm2.reshape(1, D_MODEL))


def _batcher_pairs(n):
    pairs = []
    p = 1
    while p < n:
        k = p
        while k >= 1:
            for j in range(k % p, n - k, 2 * k):
                for i in range(min(k, n - j - k)):
                    if (i + j) // (2 * p) == (i + j + k) // (2 * p):
                        pairs.append((i + j, i + j + k))
            k //= 2
        p *= 2
    return pairs


_SORT16 = _batcher_pairs(PEER_TOPK)
_BITONIC16 = [(i, i + dist) for dist in (8, 4, 2, 1) for i in range(PEER_TOPK) if not i & dist]


def _exchange(xs, i, j):
    a, b = xs[i], xs[j]
    if b is None:
        return
    if a is None:
        xs[i], xs[j] = b, None
        return
    xs[i], xs[j] = jnp.maximum(a, b), jnp.minimum(a, b)


def _sort16_desc(xs):
    xs = list(xs)
    for i, j in _SORT16:
        _exchange(xs, i, j)
    return xs


def _merge_top16(xs, ys):
    zs = []
    for j in range(PEER_TOPK):
        a, b = xs[j], ys[PEER_TOPK - 1 - j]
        zs.append(b if a is None else a if b is None else jnp.maximum(a, b))
    for i, j in _BITONIC16:
        _exchange(zs, i, j)
    return zs


_CAND_PAIRS = [(a, b) for a in range(PEER_TOPK) for b in range(PEER_TOPK) if (a + 1) * (b + 1) <= PEER_TOPK]


def _peer_route(hnt_ref, wq_ref, sk1_ref, sk2_ref, n1_ref, s2_ref, e1_ref, r2_ref, e2_ref, top1_ref, top2_ref):
    hnt = hnt_ref[...]
    for half, (sk_ref, s_ref, top_ref) in enumerate(((sk1_ref, n1_ref, top1_ref), (sk2_ref, s2_ref, top2_ref))):
        for h in range(PEER_HEADS):
            r0 = (2 * h + half) * PEER_HALF
            qh = jnp.dot(wq_ref[r0:r0 + PEER_HALF, :], hnt, preferred_element_type=f32).astype(bf16)
            st = jnp.dot(sk_ref[...], qh, preferred_element_type=f32)
            s_ref[h] = st
            xs = _sort16_desc([st[j * SUBLANES:(j + 1) * SUBLANES, :] for j in range(N_SUBKEYS // SUBLANES)])
            for shift in (4, 2, 1):
                xs = _merge_top16(xs, [pltpu.roll(x, shift, 0) for x in xs])
            for a in range(PEER_TOPK):
                top_ref[a, h:h + 1, :] = xs[a][0:1, :]
    v1 = [top1_ref[a] for a in range(PEER_TOPK)]
    v2 = [top2_ref[a] for a in range(PEER_TOPK)]
    cands = [v1[a] + v2[b] for a, b in _CAND_PAIRS]
    cands += [None] * (-len(cands) % PEER_TOPK)
    groups = [_sort16_desc(cands[i::len(cands) // PEER_TOPK]) for i in range(len(cands) // PEER_TOPK)]
    while len(groups) > 1:
        groups = [_merge_top16(groups[i], groups[i + 1]) for i in range(0, len(groups), 2)]
    sc = groups[0]
    den = jnp.zeros_like(sc[0])
    for k in range(PEER_TOPK):
        den = den + jnp.exp(sc[k] - sc[0])
    inv_den = 1.0 / den
    tau = sc[PEER_TOPK - 1]
    tile = 2 * SUBLANES

    def count_tile(j, carry):
        rows = pl.ds(pl.multiple_of(j * tile, tile), tile)
        for h in range(PEER_HEADS):
            head = slice(h, h + 1)
            s1, s2 = n1_ref[h, rows, :], s2_ref[h, rows, :]
            e1_ref[h, rows, :] = jnp.exp(s1 - top1_ref[0, head, :]) * inv_den[head, :]
            e2_ref[h, rows, :] = jnp.exp(s2 - top2_ref[0, head, :]).astype(bf16)
            reach = jnp.full_like(s1, float(PEER_TOPK))
            rank = jnp.full_like(s2, float(PEER_TOPK))
            for b in reversed(range(PEER_TOPK)):
                best = top2_ref[b, head, :]
                reach = jnp.where(s1 + best < tau[head, :], float(b), reach)
                rank = jnp.where(best <= s2, float(b), rank)
            n1_ref[h, rows, :] = reach
            r2_ref[h, rows, :] = rank.astype(bf16)
        return carry

    lax.fori_loop(0, N_SUBKEYS // tile, count_tile, 0)


def _gelu_exact(x):
    return 0.5 * x * (1.0 + lax.erf(x * (2.0 ** -0.5)))


def _peer_kernel(hnt_ref, h_ref, wq_ref, sk1_ref, sk2_ref, u_ref, vt_ref, y_ref,
                 n1_ref, s2_ref, e1_ref, r2_ref, e2_ref, top1_ref, top2_ref, hu_ref, act_ref, acc_ref):
    eb = pl.program_id(1)
    n_sub = PEER_EB // PEER_SUB
    blocks = PEER_SUB // N_SUBKEYS

    @pl.when(eb == 0)
    def _():
        _peer_route(hnt_ref, wq_ref, sk1_ref, sk2_ref, n1_ref, s2_ref, e1_ref, r2_ref, e2_ref, top1_ref, top2_ref)
        acc_ref[...] = jnp.zeros_like(acc_ref)

    def expert_rows(k):
        return slice(k * PEER_SUB, (k + 1) * PEER_SUB)

    def project_sub(k):
        hu_ref[k % 2] = jnp.dot(u_ref[expert_rows(k), :], hnt_ref[...], preferred_element_type=f32)

    def activate_sub(k):
        for i in range(blocks):
            local = k * blocks + i
            group = pl.ds(pl.multiple_of(eb * (PEER_EB // N_SUBKEYS) + (local // SUBLANES) * SUBLANES, SUBLANES),
                          SUBLANES)
            sub = local % SUBLANES
            for c in range(PEER_T // LANES):
                tok = slice(c * LANES, (c + 1) * LANES)
                for part in range(N_SUBKEYS // PEER_ROWS):
                    i2 = slice(part * PEER_ROWS, (part + 1) * PEER_ROWS)
                    rows = slice(i * N_SUBKEYS + part * PEER_ROWS, i * N_SUBKEYS + (part + 1) * PEER_ROWS)
                    g = jnp.zeros((PEER_ROWS, LANES), bf16)
                    for h in range(PEER_HEADS):
                        n1_row = n1_ref[h, group, tok][sub:sub + 1, :].astype(bf16)
                        e1_row = e1_ref[h, group, tok][sub:sub + 1, :].astype(bf16)
                        w = e1_row * e2_ref[h, i2, tok]
                        g = g + jnp.where(r2_ref[h, i2, tok] < n1_row, w, jnp.zeros_like(w))
                    act = g.astype(f32) * _gelu_exact(hu_ref[k % 2, rows, tok])
                    act_ref[k % 2, rows, tok] = act.astype(bf16)

    def combine_sub(k):
        acc_ref[...] += jnp.dot(vt_ref[:, expert_rows(k)], act_ref[k % 2], preferred_element_type=f32)

    project_sub(0)
    for k in range(n_sub):
        if k + 1 < n_sub:
            project_sub(k + 1)
        activate_sub(k)
        combine_sub(k)

    @pl.when(eb == pl.num_programs(1) - 1)
    def _():
        y_ref[...] = h_ref[...] + acc_ref[...].T


def _peer(hnt, h, wq_t, sk1, sk2, u_bf, vt_bf):
    n = h.shape[0]
    t = PEER_T
    assert n % t == 0
    return pl.pallas_call(
        _peer_kernel,
        out_shape=jax.ShapeDtypeStruct((n, D_MODEL), f32),
        grid=(n // t, N_EXPERTS // PEER_EB),
        in_specs=[pl.BlockSpec((D_MODEL, t), lambda j, e: (0, j)),
                  pl.BlockSpec((t, D_MODEL), lambda j, e: (j, 0)),
                  _const_spec((2 * PEER_HEADS * PEER_HALF, D_MODEL)),
                  _const_spec((N_SUBKEYS, PEER_HALF)), _const_spec((N_SUBKEYS, PEER_HALF)),
                  pl.BlockSpec((PEER_EB, D_MODEL), lambda j, e: (e, 0)),
                  pl.BlockSpec((D_MODEL, PEER_EB), lambda j, e: (0, e))],
        out_specs=pl.BlockSpec((t, D_MODEL), lambda j, e: (j, 0)),
        scratch_shapes=[pltpu.VMEM((PEER_HEADS, N_SUBKEYS, t), f32)] * 3 + [
            pltpu.VMEM((PEER_HEADS, N_SUBKEYS, t), bf16),
            pltpu.VMEM((PEER_HEADS, N_SUBKEYS, t), bf16),
            pltpu.VMEM((PEER_TOPK, PEER_HEADS, t), f32),
            pltpu.VMEM((PEER_TOPK, PEER_HEADS, t), f32),
            pltpu.VMEM((2, PEER_SUB, t), f32),
            pltpu.VMEM((2, PEER_SUB, t), bf16),
            pltpu.VMEM((D_MODEL, t), f32)],
        compiler_params=_params(("parallel", "arbitrary")),
        name="peer",
    )(hnt, h, wq_t, sk1, sk2, u_bf, vt_bf)


def kernel(x_prompt, x_sample, cache_a1_kv, cache_a2_kv, cache_a3_kv, cache_b_kv, norm1, w_in, q_norm_a, k_norm_a,
           q_norm_b, k_norm_b, rel_bias_table, sinks_b, w_up_a, w_up_b, w_out, norm2, w_peer_q, sub_keys_1,
           sub_keys_2, expert_u, expert_v):
    batch, seq, _ = x_prompt.shape
    dec_batch, dec_seq, _ = x_sample.shape
    assert batch == 1 and dec_seq == DEC_SEQ

    w_in_bf = w_in.astype(bf16)
    wa_bf, wb_bf, wo_bf = w_up_a.astype(bf16), w_up_b.astype(bf16), w_out.astype(bf16)
    wq_t = w_peer_q.T.astype(bf16)
    sk1, sk2 = sub_keys_1.astype(bf16), sub_keys_2.astype(bf16)
    u_bf = expert_u.astype(bf16)
    vt_bf = expert_v.T.astype(bf16)
    ones_v = jnp.ones((A_HEADS * HEAD_DIM,), f32)
    gain = jnp.concatenate(
        [jnp.concatenate([jnp.tile(q_norm_a[g], A_HEADS), jnp.tile(k_norm_a[g], A_HEADS), ones_v])
         for g in range(N_A_GROUPS)]
        + [jnp.tile(q_norm_b, B_Q_HEADS), jnp.tile(k_norm_b, B_KV_HEADS), jnp.ones((B_KV_COLS,), f32)]
    ).astype(f32).reshape(1, QKV_COLS)
    lane_head = jnp.arange(MXU_COLS) // HEAD_DIM
    seg = (lane_head[:, None] == lane_head[None, :]).astype(bf16)
    b_heads = slice(N_A_GROUPS * A_HEADS, N_A_GROUPS * A_HEADS + B_Q_HEADS)
    tables_a = [rel_bias_table[:, g * A_HEADS:(g + 1) * A_HEADS] for g in range(N_A_GROUPS)]
    table_b = rel_bias_table[:, b_heads]
    sinks = sinks_b.astype(f32)

    xp = x_prompt.reshape(seq, D_MODEL)
    qkv_p, gates_p = _project(xp, norm1, w_in_bf, gain, seg)
    outs_p, lses_p = [], []
    for g, d in enumerate(A_DILATIONS):
        o, l = _band_attend_a(qkv_p, g, d, _band_bias(tables_a[g], d, A_SPAN))
        outs_p.append(o)
        lses_p.append(l)
    out_b_p = _band_attend_b(qkv_p, _band_bias(table_b, 1, B_WINDOW), sinks)
    h_p, hnt_p = _finish(xp, outs_p, lses_p, out_b_p, gates_p, wa_bf, wb_bf, wo_bf, norm2)
    y_prompt = _peer(hnt_p, h_p, wq_t, sk1, sk2, u_bf, vt_bf).reshape(batch, seq, D_MODEL)

    new_p = []
    for g, d in enumerate(A_DILATIONS):
        lp = min(A_SPAN * d, seq)
        c0 = g * A_GROUP_COLS + A_HEADS * HEAD_DIM
        new_p.append(qkv_p[seq - lp:, c0:c0 + 2 * A_HEADS * HEAD_DIM].reshape(batch, lp, 2, A_HEADS, HEAD_DIM))
    lpb = min(B_WINDOW, seq)
    new_b_prompt = qkv_p[seq - lpb:, A_COLS + B_Q_COLS:].reshape(batch, lpb, 2, B_KV_HEADS, HEAD_DIM)

    xs = x_sample.reshape(dec_batch * dec_seq, D_MODEL)
    qkv_s, gates_s = _project(xs, norm1, w_in_bf, gain, seg)
    aw = A_HEADS * HEAD_DIM
    outs_s, lses_s, new_s = [], [], []
    cfg_a = tuple((0, h * HEAD_DIM, h * HEAD_DIM) for h in range(A_HEADS))
    for g, (d, buf) in enumerate(zip(A_DILATIONS, (cache_a1_kv, cache_a2_kv, cache_a3_kv))):
        buf_len = buf.shape[1]
        bo, bn = _sample_bias(tables_a[g], d, A_SPAN, buf_len)
        o, l, nb = _sample_attend(qkv_s, _position_minor(buf), [(aw, 3 * g)], (aw, 3 * g + 1), (aw, 3 * g + 2),
                                  aw, cfg_a, bo, bn, None, True, f"sample_a{g}")
        outs_s.append(o)
        lses_s.append(l)
        new_s.append(_position_major(nb, buf.shape))
    buf_len = cache_b_kv.shape[1]
    bo, bn = _sample_bias(table_b, 1, B_WINDOW, buf_len)
    cfg_b = tuple((h // 4, (h % 4) * HEAD_DIM, (h // 4) * HEAD_DIM) for h in range(B_Q_HEADS))
    qw = B_Q_COLS // 2
    k_blk = (A_COLS + B_Q_COLS) // B_KV_COLS
    out_b_s, new_b = _sample_attend(qkv_s, _position_minor(cache_b_kv),
                                    [(qw, A_COLS // qw), (qw, A_COLS // qw + 1)], (B_KV_COLS, k_blk),
                                    (B_KV_COLS, k_blk + 1), B_KV_COLS, cfg_b, bo, bn, sinks, False, "sample_b")
    new_b_sample = _position_major(new_b, cache_b_kv.shape)
    h_s, hnt_s = _finish(xs, outs_s, lses_s, out_b_s, gates_s, wa_bf, wb_bf, wo_bf, norm2)
    y_sample = _peer(hnt_s, h_s, wq_t, sk1, sk2, u_bf, vt_bf).reshape(dec_batch, dec_seq, D_MODEL)

    return (y_prompt, y_sample, new_p[0], new_p[1], new_p[2], new_b_prompt,
            new_s[0], new_s[1], new_s[2], new_b_sample)
```

```python
import functools
import math

import jax
import jax.numpy as jnp
from jax import lax
from jax.experimental import pallas as pl
from jax.experimental.pallas import tpu as pltpu

f32 = jnp.float32
bf16 = jnp.bfloat16

D_MODEL = 1024
HEAD_DIM = 64
A_SPAN = 128
A_DILATIONS = (1, 4, 16)
A_HEADS = 4
N_A_GROUPS = 3
B_WINDOW = 128
B_Q_HEADS = 8
B_KV_HEADS = 2
N_BUCKETS = 32
MAX_DISTANCE = 2048
BLK = 128
DEC_SEQ = 8
A_GROUP_COLS = 3 * A_HEADS * HEAD_DIM
A_COLS = N_A_GROUPS * A_GROUP_COLS
B_Q_COLS = B_Q_HEADS * HEAD_DIM
B_KV_COLS = B_KV_HEADS * HEAD_DIM
QKV_COLS = A_COLS + B_Q_COLS + 2 * B_KV_COLS
GATE_COLS = 2 * D_MODEL
N_SUBKEYS = 128
N_EXPERTS = N_SUBKEYS * N_SUBKEYS
PEER_HEADS = 8
PEER_TOPK = 16
PEER_HALF = 128
EPS = 1e-6
NEG = -1e30
SCALE = HEAD_DIM ** -0.5

LANES = 128
SUBLANES = 8
MXU_COLS = 256
VMEM_LIMIT = 56 * 1024 * 1024

PROJ_TM = 256
PEER_T = 512
PEER_EB = 2048
PEER_SUB = 512
assert (PEER_EB // N_SUBKEYS) % SUBLANES == 0 and PEER_EB % PEER_SUB == 0 and PEER_SUB % N_SUBKEYS == 0


def _const_spec(shape):
    nd = len(shape)
    return pl.BlockSpec(shape, lambda *_: (0,) * nd, pipeline_mode=pl.Buffered(1))


def _params(sem):
    return pltpu.CompilerParams(dimension_semantics=sem, vmem_limit_bytes=VMEM_LIMIT)


_NORM_ALL, _NORM_NONE, _NORM_FIRST_HALF = 0, 1, 2


def _chunk_norm_mode(c):
    col = c * MXU_COLS
    if col < A_COLS:
        return _NORM_NONE if (col % A_GROUP_COLS) == 2 * A_HEADS * HEAD_DIM else _NORM_ALL
    if col < A_COLS + B_Q_COLS:
        return _NORM_ALL
    return _NORM_FIRST_HALF


def _proj_kernel(x_ref, g1_ref, w_ref, gain_ref, seg_ref, qkv_ref, gate_ref):
    x = x_ref[...]
    ms = jnp.mean(x * x, axis=-1, keepdims=True)
    xn = ((x * lax.rsqrt(ms + EPS)) * g1_ref[...]).astype(bf16)
    seg = seg_ref[...]
    for c in range(QKV_COLS // MXU_COLS):
        cols = slice(c * MXU_COLS, (c + 1) * MXU_COLS)
        hb = jnp.dot(xn, w_ref[:, cols], preferred_element_type=f32)
        mode = _chunk_norm_mode(c)
        if mode != _NORM_NONE:
            h2 = hb * hb
            hi = h2.astype(bf16)
            lo = (h2 - hi.astype(f32)).astype(bf16)
            ss = (jnp.dot(hi, seg, preferred_element_type=f32)
                  + jnp.dot(lo, seg, preferred_element_type=f32))
            normed = (hb * lax.rsqrt(ss * (1.0 / HEAD_DIM) + EPS)) * gain_ref[:, cols]
            if mode == _NORM_ALL:
                hb = normed
            else:
                hb = jnp.concatenate([normed[:, :LANES], hb[:, LANES:]], axis=1)
        qkv_ref[:, cols] = hb
    hg = jnp.dot(xn, w_ref[:, QKV_COLS:], preferred_element_type=f32)
    gate_ref[...] = jax.nn.sigmoid(hg)


def _project(x, norm1, w_in_bf, gain, seg):
    n = x.shape[0]
    tm = PROJ_TM
    assert n % tm == 0
    return pl.pallas_call(
        _proj_kernel,
        out_shape=(jax.ShapeDtypeStruct((n, QKV_COLS), f32), jax.ShapeDtypeStruct((n, GATE_COLS), f32)),
        grid=(n // tm,),
        in_specs=[
            pl.BlockSpec((tm, D_MODEL), lambda i: (i, 0)),
            _const_spec((1, D_MODEL)),
            _const_spec((D_MODEL, QKV_COLS + GATE_COLS)),
            _const_spec((1, QKV_COLS)),
            _const_spec((MXU_COLS, MXU_COLS)),
        ],
        out_specs=(pl.BlockSpec((tm, QKV_COLS), lambda i: (i, 0)),
                   pl.BlockSpec((tm, GATE_COLS), lambda i: (i, 0))),
        compiler_params=_params(("parallel",)),
        name="project",
    )(x, norm1.reshape(1, D_MODEL), w_in_bf, gain, seg)


def _t5_bucket(dist):
    exact = N_BUCKETS // 2
    d32 = jnp.maximum(dist, 1).astype(f32)
    large = exact + (jnp.log(d32 / exact) / math.log(MAX_DISTANCE / exact) * (N_BUCKETS - exact)).astype(jnp.int32)
    large = jnp.minimum(large, N_BUCKETS - 1)
    return jnp.where(dist < exact, dist, large)


def _toeplitz(period_vals, rows, width):
    reps = -(-rows * width // (width + 1))
    return jnp.tile(period_vals, (1, reps))[:, :rows * width].reshape(-1, rows, width)


def _band_bias(table_h, d, span):
    k = jnp.arange(2 * BLK + 1)
    steps = BLK - k
    vals = table_h[_t5_bucket(jnp.maximum(steps, 0) * d)].astype(f32).T
    vals = jnp.where(((steps >= 0) & (steps <= span))[None], vals, NEG)
    return _toeplitz(vals, BLK, 2 * BLK)


def _sample_bias(table_h, d, steps, buf_len):
    assert buf_len == steps * d
    width = buf_len + DEC_SEQ
    dist = buf_len - jnp.arange(width + 1)
    ok = (dist >= 0) & (dist % d == 0)
    vals = table_h[_t5_bucket(jnp.maximum(dist, 0))].astype(f32).T
    vals = jnp.where(ok[None], vals, NEG)
    b = _toeplitz(vals, DEC_SEQ, width).reshape(-1, width)
    old = b[:, :buf_len]
    new = jnp.pad(b[:, buf_len:], ((0, 0), (0, LANES - DEC_SEQ)), constant_values=NEG)
    return old, new


def _band_kernel(*refs, n_q, n_kv, head_cfg, has_sink, want_lse, dilation):
    q_refs = refs[:n_q]
    pos = n_q
    kp_refs, kc_refs, vp_refs, vc_refs = (refs[pos + i * n_kv:pos + (i + 1) * n_kv] for i in range(4))
    pos += 4 * n_kv
    bias_ref = refs[pos]
    pos += 1
    sink_ref = None
    if has_sink:
        sink_ref = refs[pos]
        pos += 1
    o_ref = refs[pos]
    pos += 1
    l_ref = None
    if want_lse:
        l_ref = refs[pos]
        pos += 1
    o_stage = refs[pos:pos + n_q] if dilation > 1 else None
    l_stage = refs[pos + n_q:pos + 2 * n_q] if dilation > 1 and want_lse else None

    blk = pl.program_id(0)
    col = lax.broadcasted_iota(jnp.int32, (BLK, 2 * BLK), 1)
    no_prev = jnp.where((col < BLK) & (blk == 0), NEG, 0.0).astype(f32)
    contract_last = (((1,), (1,)), ((), ()))
    heads_per_ref = LANES // HEAD_DIM

    def residue(r, carry):
        rows = pl.ds(r, BLK, stride=dilation) if dilation > 1 else slice(None)
        qs = [ref[rows, :] for ref in q_refs]
        kps, kcs = [ref[rows, :] for ref in kp_refs], [ref[rows, :] for ref in kc_refs]
        vps, vcs = [ref[rows, :] for ref in vp_refs], [ref[rows, :] for ref in vc_refs]
        outs, lses = [], []
        for h, (qi, qoff, ki, koff) in enumerate(head_cfg):
            q = qs[qi][:, qoff:qoff + HEAD_DIM].astype(bf16)
            k = jnp.concatenate([kps[ki][:, koff:koff + HEAD_DIM], kcs[ki][:, koff:koff + HEAD_DIM]],
                                axis=0).astype(bf16)
            v = jnp.concatenate([vps[ki][:, koff:koff + HEAD_DIM], vcs[ki][:, koff:koff + HEAD_DIM]],
                                axis=0).astype(bf16)
            s = lax.dot_general(q, k, contract_last, preferred_element_type=f32) * SCALE + bias_ref[h] + no_prev
            m = jnp.max(s, axis=-1, keepdims=True)
            if has_sink:
                sink = sink_ref[h]
                m = jnp.maximum(m, sink)
            e = jnp.exp(s - m)
            den = jnp.sum(e, axis=-1, keepdims=True)
            if has_sink:
                den = den + jnp.exp(sink - m)
            p = (e / den).astype(bf16)
            outs.append(jnp.dot(p, v, preferred_element_type=f32))
            if want_lse:
                lses.append(jnp.broadcast_to(m + jnp.log(den), (BLK, HEAD_DIM)))
        for j in range(n_q):
            o = jnp.concatenate(outs[j * heads_per_ref:(j + 1) * heads_per_ref], axis=1)
            if dilation > 1:
                o_stage[j][rows, :] = o
            else:
                o_ref[:, j * LANES:(j + 1) * LANES] = o
            if want_lse:
                lse = jnp.concatenate(lses[j * heads_per_ref:(j + 1) * heads_per_ref], axis=1)
                if dilation > 1:
                    l_stage[j][rows, :] = lse
                else:
                    l_ref[:, j * LANES:(j + 1) * LANES] = lse
        return carry

    if dilation > 1:
        lax.fori_loop(0, dilation, residue, 0)
        for j in range(n_q):
            o_ref[:, j * LANES:(j + 1) * LANES] = o_stage[j][...]
            if want_lse:
                l_ref[:, j * LANES:(j + 1) * LANES] = l_stage[j][...]
    else:
        residue(0, 0)


def _band_attend_a(qkv, g, d, bias):
    s_len = qkv.shape[0]
    rows = d * BLK
    assert s_len % rows == 0
    width = A_HEADS * HEAD_DIM
    n_ref = width // LANES
    base = g * 3 * n_ref

    def specs(part, prev):
        if prev:
            return [pl.BlockSpec((rows, LANES), lambda b, c=base + part * n_ref + j: (jnp.maximum(b - 1, 0), c))
                    for j in range(n_ref)]
        return [pl.BlockSpec((rows, LANES), lambda b, c=base + part * n_ref + j: (b, c)) for j in range(n_ref)]

    heads_per_ref = LANES // HEAD_DIM
    head_cfg = tuple((h // heads_per_ref, (h % heads_per_ref) * HEAD_DIM) * 2 for h in range(A_HEADS))
    in_specs = specs(0, False) + specs(1, True) + specs(1, False) + specs(2, True) + specs(2, False)
    scratch = [pltpu.VMEM((rows, LANES), f32)] * (2 * n_ref) if d > 1 else []
    return pl.pallas_call(
        functools.partial(_band_kernel, n_q=n_ref, n_kv=n_ref, head_cfg=head_cfg, has_sink=False, want_lse=True,
                          dilation=d),
        out_shape=(jax.ShapeDtypeStruct((s_len, width), f32), jax.ShapeDtypeStruct((s_len, width), f32)),
        grid=(s_len // rows,),
        in_specs=in_specs + [_const_spec((A_HEADS, BLK, 2 * BLK))],
        out_specs=(pl.BlockSpec((rows, width), lambda b: (b, 0)),
                   pl.BlockSpec((rows, width), lambda b: (b, 0))),
        scratch_shapes=scratch,
        compiler_params=_params(("parallel",)),
        name=f"band_a{g}",
    )(*([qkv] * len(in_specs)), bias)


def _band_attend_b(qkv, bias, sinks):
    s_len = qkv.shape[0]
    assert s_len % BLK == 0
    assert B_KV_COLS == LANES
    n_q = B_Q_COLS // LANES
    q_blk = A_COLS // LANES
    k_blk = (A_COLS + B_Q_COLS) // LANES
    heads_per_ref = LANES // HEAD_DIM
    group = B_Q_HEADS // B_KV_HEADS
    head_cfg = tuple((h // heads_per_ref, (h % heads_per_ref) * HEAD_DIM, 0, (h // group) * HEAD_DIM)
                     for h in range(B_Q_HEADS))

    def kv_spec(off, prev):
        if prev:
            return pl.BlockSpec((BLK, LANES), lambda b: (jnp.maximum(b - 1, 0), k_blk + off))
        return pl.BlockSpec((BLK, LANES), lambda b: (b, k_blk + off))

    in_specs = [pl.BlockSpec((BLK, LANES), lambda b, c=q_blk + j: (b, c)) for j in range(n_q)]
    in_specs += [kv_spec(0, True), kv_spec(0, False), kv_spec(1, True), kv_spec(1, False)]
    return pl.pallas_call(
        functools.partial(_band_kernel, n_q=n_q, n_kv=1, head_cfg=head_cfg, has_sink=True, want_lse=False,
                          dilation=1),
        out_shape=jax.ShapeDtypeStruct((s_len, B_Q_COLS), f32),
        grid=(s_len // BLK,),
        in_specs=in_specs + [_const_spec((B_Q_HEADS, BLK, 2 * BLK)), pl.BlockSpec(memory_space=pltpu.SMEM)],
        out_specs=pl.BlockSpec((BLK, B_Q_COLS), lambda b: (b, 0)),
        compiler_params=_params(("parallel",)),
        name="band_b",
    )(*([qkv] * len(in_specs)), bias, sinks)


def _sample_kernel(*refs, groups):
    n_in = sum(n_q + 5 + int(has_sink) for (n_q, _, has_sink, _, _) in groups)
    ins, outs = list(refs[:n_in]), list(refs[n_in:])
    for (n_q, head_cfg, has_sink, want_lse, kv_cols) in groups:
        q_refs = [ins.pop(0) for _ in range(n_q)]
        kn_ref, vn_ref, buf_ref, bo_ref, bn_ref = (ins.pop(0) for _ in range(5))
        sink_ref = ins.pop(0) if has_sink else None
        o_ref = outs.pop(0)
        l_ref = outs.pop(0) if want_lse else None
        nb_ref = outs.pop(0)
        _sample_group(q_refs, kn_ref, vn_ref, buf_ref, bo_ref, bn_ref, sink_ref, o_ref, l_ref, nb_ref,
                      head_cfg, kv_cols)


def _sample_group(q_refs, kn_ref, vn_ref, buf_ref, bo_ref, bn_ref, sink_ref, o_ref, l_ref, nb_ref, head_cfg, kv_cols):
    has_sink = sink_ref is not None
    want_lse = l_ref is not None

    buf_len = buf_ref.shape[2]
    t_new = DEC_SEQ
    k_new = kn_ref[...]
    v_new = vn_ref[...]
    pad = jnp.zeros((LANES - t_new, kv_cols), f32)
    k_pad = jnp.concatenate([k_new, pad], axis=0)
    v_pad = jnp.concatenate([v_new, pad], axis=0)
    buf = buf_ref[0]

    new_t = jnp.concatenate([k_pad, v_pad], axis=1).T
    shifted = pltpu.roll(buf, buf_len - t_new, 1)
    lane = lax.broadcasted_iota(jnp.int32, (2 * kv_cols, LANES), 1)
    tail = jnp.where(lane >= LANES - t_new, pltpu.roll(new_t, LANES - t_new, 1), shifted[:, buf_len - LANES:])
    if buf_len > LANES:
        nb_ref[0, :, 0:buf_len - LANES] = shifted[:, 0:buf_len - LANES]
    nb_ref[0, :, buf_len - LANES:buf_len] = tail

    rows = []
    for (qi, qoff, kvoff) in head_cfg:
        q = q_refs[qi][:, qoff:qoff + HEAD_DIM]
        pieces = []
        if kvoff > 0:
            pieces.append(jnp.zeros((t_new, kvoff), f32))
        pieces.append(q)
        if kv_cols - kvoff - HEAD_DIM > 0:
            pieces.append(jnp.zeros((t_new, kv_cols - kvoff - HEAD_DIM), f32))
        rows.append(jnp.concatenate(pieces, axis=1) if len(pieces) > 1 else q)
    qrows = jnp.concatenate(rows, axis=0).astype(bf16)

    kt_old = buf[0:kv_cols, :].astype(bf16)
    vt_old = buf[kv_cols:2 * kv_cols, :].astype(bf16)
    k_pad = k_pad.astype(bf16)
    v_pad = v_pad.astype(bf16)

    contract_last = (((1,), (1,)), ((), ()))
    s_old = jnp.dot(qrows, kt_old, preferred_element_type=f32) * SCALE + bo_ref[...]
    s_new = lax.dot_general(qrows, k_pad, contract_last, preferred_element_type=f32) * SCALE + bn_ref[...]
    m = jnp.maximum(jnp.max(s_old, axis=-1, keepdims=True), jnp.max(s_new, axis=-1, keepdims=True))
    n_heads = len(head_cfg)
    sink_col = None
    if has_sink:
        sink_col = jnp.concatenate(
            [jnp.full((t_new, 1), sink_ref[h], f32) for h in range(n_heads)], axis=0)
        m = jnp.maximum(m, sink_col)
    e_old = jnp.exp(s_old - m)
    e_new = jnp.exp(s_new - m)
    den = jnp.sum(e_old, axis=-1, keepdims=True) + jnp.sum(e_new, axis=-1, keepdims=True)
    if has_sink:
        den = den + jnp.exp(sink_col - m)
    acc = (lax.dot_general((e_old / den).astype(bf16), vt_old, contract_last, preferred_element_type=f32)
           + jnp.dot((e_new / den).astype(bf16), v_pad, preferred_element_type=f32))
    lse = m + jnp.log(den)
    for h, (qi, qoff, kvoff) in enumerate(head_cfg):
        o_ref[:, h * HEAD_DIM:(h + 1) * HEAD_DIM] = acc[h * t_new:(h + 1) * t_new, kvoff:kvoff + HEAD_DIM]
        if want_lse:
            l_ref[:, h * HEAD_DIM:(h + 1) * HEAD_DIM] = jnp.broadcast_to(
                lse[h * t_new:(h + 1) * t_new, :], (t_new, HEAD_DIM))


def _position_minor(buf):
    bd, buf_len = buf.shape[:2]
    return jnp.transpose(buf, (0, 2, 3, 4, 1)).reshape(bd, -1, buf_len)


def _position_major(buf_t, shape):
    bd, buf_len, two, groups, hd = shape
    return jnp.transpose(buf_t.reshape(bd, two, groups, hd, buf_len), (0, 4, 1, 2, 3))


def _sample_attend(qkv, group_args):
    bd = group_args[0][0].shape[0]
    assert qkv.shape[0] == bd * DEC_SEQ

    def col_spec(width, idx):
        return pl.BlockSpec((DEC_SEQ, width), lambda b: (b, idx))

    in_specs, args, out_shape, out_specs, groups = [], [], [], [], []
    for (buf, q_blocks, k_block, v_block, kv_cols, head_cfg, bias_old, bias_new, sinks, want_lse) in group_args:
        _, width2, buf_len = buf.shape
        assert width2 == 2 * kv_cols and buf.shape[0] == bd and buf_len % LANES == 0
        rows = len(head_cfg) * DEC_SEQ
        in_specs += [col_spec(w, i) for (w, i) in q_blocks]
        in_specs += [col_spec(*k_block), col_spec(*v_block),
                     pl.BlockSpec((1, width2, buf_len), lambda b: (b, 0, 0)),
                     _const_spec((rows, buf_len)), _const_spec((rows, LANES))]
        args += [qkv] * (len(q_blocks) + 2) + [buf, bias_old, bias_new]
        if sinks is not None:
            in_specs.append(pl.BlockSpec(memory_space=pltpu.SMEM))
            args.append(sinks)
        out_w = len(head_cfg) * HEAD_DIM
        for _ in range(2 if want_lse else 1):
            out_shape.append(jax.ShapeDtypeStruct((bd * DEC_SEQ, out_w), f32))
            out_specs.append(pl.BlockSpec((DEC_SEQ, out_w), lambda b: (b, 0)))
        out_shape.append(jax.ShapeDtypeStruct(buf.shape, f32))
        out_specs.append(pl.BlockSpec((1, width2, buf_len), lambda b: (b, 0, 0)))
        groups.append((len(q_blocks), head_cfg, sinks is not None, want_lse, kv_cols))
    return pl.pallas_call(
        functools.partial(_sample_kernel, groups=tuple(groups)),
        out_shape=tuple(out_shape),
        grid=(bd,),
        in_specs=in_specs,
        out_specs=tuple(out_specs),
        compiler_params=_params(("parallel",)),
        name="sample",
    )(*args)


def _finish_kernel(x_ref, oa0_ref, oa1_ref, oa2_ref, la0_ref, la1_ref, la2_ref, ob_ref, gate_ref,
                   wa_ref, wb_ref, wo_ref, g2_ref, h_ref, hnt_ref):
    l0, l1, l2 = la0_ref[...], la1_ref[...], la2_ref[...]
    m = jnp.maximum(jnp.maximum(l0, l1), l2)
    e0, e1, e2 = jnp.exp(l0 - m), jnp.exp(l1 - m), jnp.exp(l2 - m)
    den = e0 + e1 + e2
    oa = (e0 / den) * oa0_ref[...] + (e1 / den) * oa1_ref[...] + (e2 / den) * oa2_ref[...]
    ya = jnp.dot(oa.astype(bf16), wa_ref[...], preferred_element_type=f32)
    yb = jnp.dot(ob_ref[...].astype(bf16), wb_ref[...], preferred_element_type=f32)
    merged = gate_ref[:, :D_MODEL] * ya + gate_ref[:, D_MODEL:] * yb
    h = x_ref[...] + jnp.dot(merged.astype(bf16), wo_ref[...], preferred_element_type=f32)
    h_ref[...] = h
    ms = jnp.mean(h * h, axis=-1, keepdims=True)
    hn = (h * lax.rsqrt(ms + EPS)) * g2_ref[...]
    hnt_ref[...] = hn.T.astype(bf16)


def _finish(x, outs_a, lses_a, out_b, gates, wa_bf, wb_bf, wo_bf, norm2):
    n = x.shape[0]
    tm = PROJ_TM
    assert n % tm == 0
    aw = A_HEADS * HEAD_DIM

    def row_spec(w):
        return pl.BlockSpec((tm, w), lambda i: (i, 0))

    return pl.pallas_call(
        _finish_kernel,
        out_shape=(jax.ShapeDtypeStruct((n, D_MODEL), f32), jax.ShapeDtypeStruct((D_MODEL, n), bf16)),
        grid=(n // tm,),
        in_specs=[row_spec(D_MODEL)] + [row_spec(aw)] * 6 + [row_spec(B_Q_COLS), row_spec(GATE_COLS),
                  _const_spec((aw, D_MODEL)), _const_spec((B_Q_COLS, D_MODEL)), _const_spec((D_MODEL, D_MODEL)),
                  _const_spec((1, D_MODEL))],
        out_specs=(row_spec(D_MODEL), pl.BlockSpec((D_MODEL, tm), lambda i: (0, i))),
        compiler_params=_params(("parallel",)),
        name="finish",
    )(x, *outs_a, *lses_a, out_b, gates, wa_bf, wb_bf, wo_bf, norm2.reshape(1, D_MODEL))


def _batcher_pairs(n):
    pairs = []
    p = 1
    while p < n:
        k = p
        while k >= 1:
            for j in range(k % p, n - k, 2 * k):
                for i in range(min(k, n - j - k)):
                    if (i + j) // (2 * p) == (i + j + k) // (2 * p):
                        pairs.append((i + j, i + j + k))
            k //= 2
        p *= 2
    return pairs


_SORT16 = _batcher_pairs(PEER_TOPK)
_BITONIC16 = [(i, i + dist) for dist in (8, 4, 2, 1) for i in range(PEER_TOPK) if not i & dist]


def _exchange(xs, i, j):
    a, b = xs[i], xs[j]
    if b is None:
        return
    if a is None:
        xs[i], xs[j] = b, None
        return
    xs[i], xs[j] = jnp.maximum(a, b), jnp.minimum(a, b)


def _sort16_desc(xs):
    xs = list(xs)
    for i, j in _SORT16:
        _exchange(xs, i, j)
    return xs


def _merge_top16(xs, ys):
    zs = []
    for j in range(PEER_TOPK):
        a, b = xs[j], ys[PEER_TOPK - 1 - j]
        zs.append(b if a is None else a if b is None else jnp.maximum(a, b))
    for i, j in _BITONIC16:
        _exchange(zs, i, j)
    return zs


_CAND_PAIRS = [(a, b) for a in range(PEER_TOPK) for b in range(PEER_TOPK) if (a + 1) * (b + 1) <= PEER_TOPK]


def _peer_route(hnt_ref, wq_ref, sk1_ref, sk2_ref, n1_ref, s2_ref, e1_ref, r2_ref, e2_ref, top1_ref, top2_ref):
    hnt = hnt_ref[...]
    for half, (sk_ref, s_ref, top_ref) in enumerate(((sk1_ref, n1_ref, top1_ref), (sk2_ref, s2_ref, top2_ref))):
        for h in range(PEER_HEADS):
            r0 = (2 * h + half) * PEER_HALF
            qh = jnp.dot(wq_ref[r0:r0 + PEER_HALF, :], hnt, preferred_element_type=f32).astype(bf16)
            st = jnp.dot(sk_ref[...], qh, preferred_element_type=f32)
            s_ref[h] = st
            xs = _sort16_desc([st[j * SUBLANES:(j + 1) * SUBLANES, :] for j in range(N_SUBKEYS // SUBLANES)])
            for shift in (4, 2, 1):
                xs = _merge_top16(xs, [pltpu.roll(x, shift, 0) for x in xs])
            for a in range(PEER_TOPK):
                top_ref[a, h:h + 1, :] = xs[a][0:1, :]
    v1 = [top1_ref[a] for a in range(PEER_TOPK)]
    v2 = [top2_ref[a] for a in range(PEER_TOPK)]
    cands = [v1[a] + v2[b] for a, b in _CAND_PAIRS]
    cands += [None] * (-len(cands) % PEER_TOPK)
    groups = [_sort16_desc(cands[i::len(cands) // PEER_TOPK]) for i in range(len(cands) // PEER_TOPK)]
    while len(groups) > 1:
        groups = [_merge_top16(groups[i], groups[i + 1]) for i in range(0, len(groups), 2)]
    sc = groups[0]
    den = jnp.zeros_like(sc[0])
    for k in range(PEER_TOPK):
        den = den + jnp.exp(sc[k] - sc[0])
    inv_den = 1.0 / den
    tau = sc[PEER_TOPK - 1]
    tile = 2 * SUBLANES

    def count_tile(j, carry):
        rows = pl.ds(pl.multiple_of(j * tile, tile), tile)
        for h in range(PEER_HEADS):
            head = slice(h, h + 1)
            s1, s2 = n1_ref[h, rows, :], s2_ref[h, rows, :]
            e1_ref[h, rows, :] = jnp.exp(s1 - top1_ref[0, head, :]) * inv_den[head, :]
            e2_ref[h, rows, :] = jnp.exp(s2 - top2_ref[0, head, :]).astype(bf16)
            reach = jnp.full_like(s1, float(PEER_TOPK))
            rank = jnp.full_like(s2, float(PEER_TOPK))
            for b in reversed(range(PEER_TOPK)):
                best = top2_ref[b, head, :]
                reach = jnp.where(s1 + best < tau[head, :], float(b), reach)
                rank = jnp.where(best <= s2, float(b), rank)
            n1_ref[h, rows, :] = reach
            r2_ref[h, rows, :] = rank.astype(bf16)
        return carry

    lax.fori_loop(0, N_SUBKEYS // tile, count_tile, 0)


def _gelu_exact(x):
    return 0.5 * x * (1.0 + lax.erf(x * (2.0 ** -0.5)))


def _peer_kernel(hnt_ref, h_ref, wq_ref, sk1_ref, sk2_ref, u_ref, vt_ref, y_ref,
                 n1_ref, s2_ref, e1_ref, r2_ref, e2_ref, top1_ref, top2_ref, hu_ref, act_ref, acc_ref):
    eb = pl.program_id(1)
    n_sub = PEER_EB // PEER_SUB
    blocks = PEER_SUB // N_SUBKEYS

    @pl.when(eb == 0)
    def _():
        _peer_route(hnt_ref, wq_ref, sk1_ref, sk2_ref, n1_ref, s2_ref, e1_ref, r2_ref, e2_ref, top1_ref, top2_ref)
        acc_ref[...] = jnp.zeros_like(acc_ref)

    def project_sub(k, slot):
        hu_ref[slot] = jnp.dot(u_ref[k], hnt_ref[...], preferred_element_type=f32)

    def activate_sub(k, slot):
        for i in range(blocks):
            local = k * blocks + i
            base = eb * (PEER_EB // N_SUBKEYS) + (local // SUBLANES) * SUBLANES
            group = pl.ds(pl.multiple_of(base, SUBLANES), SUBLANES)
            to_top = (SUBLANES - local % SUBLANES) % SUBLANES
            rows = slice(i * N_SUBKEYS, (i + 1) * N_SUBKEYS)
            for c in range(PEER_T // LANES):
                tok = slice(c * LANES, (c + 1) * LANES)
                g = jnp.zeros((N_SUBKEYS, LANES), bf16)
                for h in range(PEER_HEADS):
                    n1_row = pltpu.roll(n1_ref[h, group, tok], to_top, 0)[0:1, :].astype(bf16)
                    e1_row = pltpu.roll(e1_ref[h, group, tok], to_top, 0)[0:1, :].astype(bf16)
                    w = e1_row * e2_ref[h, :, tok]
                    g = g + jnp.where(r2_ref[h, :, tok] < n1_row, w, jnp.zeros_like(w))
                act = g.astype(f32) * _gelu_exact(hu_ref[slot, rows, tok])
                act_ref[slot, rows, tok] = act.astype(bf16)

    def combine_sub(k, slot):
        acc_ref[...] += jnp.dot(vt_ref[k], act_ref[slot], preferred_element_type=f32)

    project_sub(0, 0)
    for k in range(n_sub):
        if k + 1 < n_sub:
            project_sub(k + 1, (k + 1) % 2)
        activate_sub(k, k % 2)
        combine_sub(k, k % 2)

    @pl.when(eb == pl.num_programs(1) - 1)
    def _():
        y_ref[...] = h_ref[...] + acc_ref[...].T


def _peer(hnt, h, wq_t, sk1, sk2, u_bf, vt_bf):
    n = h.shape[0]
    t = PEER_T
    assert n % t == 0
    return pl.pallas_call(
        _peer_kernel,
        out_shape=jax.ShapeDtypeStruct((n, D_MODEL), f32),
        grid=(n // t, N_EXPERTS // PEER_EB),
        in_specs=[pl.BlockSpec((D_MODEL, t), lambda j, e: (0, j)),
                  pl.BlockSpec((t, D_MODEL), lambda j, e: (j, 0)),
                  _const_spec((2 * PEER_HEADS * PEER_HALF, D_MODEL)),
                  _const_spec((N_SUBKEYS, PEER_HALF)), _const_spec((N_SUBKEYS, PEER_HALF)),
                  pl.BlockSpec((PEER_EB // PEER_SUB, PEER_SUB, D_MODEL), lambda j, e: (e, 0, 0)),
                  pl.BlockSpec((PEER_EB // PEER_SUB, D_MODEL, PEER_SUB), lambda j, e: (e, 0, 0))],
        out_specs=pl.BlockSpec((t, D_MODEL), lambda j, e: (j, 0)),
        scratch_shapes=[pltpu.VMEM((PEER_HEADS, N_SUBKEYS, t), f32)] * 3 + [
            pltpu.VMEM((PEER_HEADS, N_SUBKEYS, t), bf16),
            pltpu.VMEM((PEER_HEADS, N_SUBKEYS, t), bf16),
            pltpu.VMEM((PEER_TOPK, PEER_HEADS, t), f32),
            pltpu.VMEM((PEER_TOPK, PEER_HEADS, t), f32),
            pltpu.VMEM((2, PEER_SUB, t), f32),
            pltpu.VMEM((2, PEER_SUB, t), bf16),
            pltpu.VMEM((D_MODEL, t), f32)],
        compiler_params=_params(("parallel", "arbitrary")),
        name="peer",
    )(hnt, h, wq_t, sk1, sk2, u_bf, vt_bf)


def kernel(x_prompt, x_sample, cache_a1_kv, cache_a2_kv, cache_a3_kv, cache_b_kv, norm1, w_in, q_norm_a, k_norm_a,
           q_norm_b, k_norm_b, rel_bias_table, sinks_b, w_up_a, w_up_b, w_out, norm2, w_peer_q, sub_keys_1,
           sub_keys_2, expert_u, expert_v):
    batch, seq, _ = x_prompt.shape
    dec_batch, dec_seq, _ = x_sample.shape
    assert batch == 1 and dec_seq == DEC_SEQ

    w_in_bf = w_in.astype(bf16)
    wa_bf, wb_bf, wo_bf = w_up_a.astype(bf16), w_up_b.astype(bf16), w_out.astype(bf16)
    wq_t = w_peer_q.T.astype(bf16)
    sk1, sk2 = sub_keys_1.astype(bf16), sub_keys_2.astype(bf16)
    u_bf = expert_u.astype(bf16).reshape(N_EXPERTS // PEER_SUB, PEER_SUB, D_MODEL)
    vt_bf = jnp.transpose(expert_v.reshape(N_EXPERTS // PEER_SUB, PEER_SUB, D_MODEL), (0, 2, 1)).astype(bf16)
    ones_v = jnp.ones((A_HEADS * HEAD_DIM,), f32)
    gain = jnp.concatenate(
        [jnp.concatenate([jnp.tile(q_norm_a[g], A_HEADS), jnp.tile(k_norm_a[g], A_HEADS), ones_v])
         for g in range(N_A_GROUPS)]
        + [jnp.tile(q_norm_b, B_Q_HEADS), jnp.tile(k_norm_b, B_KV_HEADS), jnp.ones((B_KV_COLS,), f32)]
    ).astype(f32).reshape(1, QKV_COLS)
    lane_head = jnp.arange(MXU_COLS) // HEAD_DIM
    seg = (lane_head[:, None] == lane_head[None, :]).astype(bf16)
    b_heads = slice(N_A_GROUPS * A_HEADS, N_A_GROUPS * A_HEADS + B_Q_HEADS)
    tables_a = [rel_bias_table[:, g * A_HEADS:(g + 1) * A_HEADS] for g in range(N_A_GROUPS)]
    table_b = rel_bias_table[:, b_heads]
    sinks = sinks_b.astype(f32)

    xp = x_prompt.reshape(seq, D_MODEL)
    qkv_p, gates_p = _project(xp, norm1, w_in_bf, gain, seg)
    outs_p, lses_p = [], []
    for g, d in enumerate(A_DILATIONS):
        o, l = _band_attend_a(qkv_p, g, d, _band_bias(tables_a[g], d, A_SPAN))
        outs_p.append(o)
        lses_p.append(l)
    out_b_p = _band_attend_b(qkv_p, _band_bias(table_b, 1, B_WINDOW), sinks)
    h_p, hnt_p = _finish(xp, outs_p, lses_p, out_b_p, gates_p, wa_bf, wb_bf, wo_bf, norm2)
    y_prompt = _peer(hnt_p, h_p, wq_t, sk1, sk2, u_bf, vt_bf).reshape(batch, seq, D_MODEL)

    new_p = []
    for g, d in enumerate(A_DILATIONS):
        lp = min(A_SPAN * d, seq)
        c0 = g * A_GROUP_COLS + A_HEADS * HEAD_DIM
        new_p.append(qkv_p[seq - lp:, c0:c0 + 2 * A_HEADS * HEAD_DIM].reshape(batch, lp, 2, A_HEADS, HEAD_DIM))
    lpb = min(B_WINDOW, seq)
    new_b_prompt = qkv_p[seq - lpb:, A_COLS + B_Q_COLS:].reshape(batch, lpb, 2, B_KV_HEADS, HEAD_DIM)

    xs = x_sample.reshape(dec_batch * dec_seq, D_MODEL)
    qkv_s, gates_s = _project(xs, norm1, w_in_bf, gain, seg)
    aw = A_HEADS * HEAD_DIM
    caches_a = (cache_a1_kv, cache_a2_kv, cache_a3_kv)
    cfg_a = tuple((0, h * HEAD_DIM, h * HEAD_DIM) for h in range(A_HEADS))
    group_args = []
    for g, (d, buf) in enumerate(zip(A_DILATIONS, caches_a)):
        bo, bn = _sample_bias(tables_a[g], d, A_SPAN, buf.shape[1])
        group_args.append((_position_minor(buf), [(aw, 3 * g)], (aw, 3 * g + 1), (aw, 3 * g + 2), aw, cfg_a,
                           bo, bn, None, True))
    bo, bn = _sample_bias(table_b, 1, B_WINDOW, cache_b_kv.shape[1])
    cfg_b = tuple((h // 4, (h % 4) * HEAD_DIM, (h // 4) * HEAD_DIM) for h in range(B_Q_HEADS))
    qw = B_Q_COLS // 2
    k_blk = (A_COLS + B_Q_COLS) // B_KV_COLS
    group_args.append((_position_minor(cache_b_kv), [(qw, A_COLS // qw), (qw, A_COLS // qw + 1)],
                       (B_KV_COLS, k_blk), (B_KV_COLS, k_blk + 1), B_KV_COLS, cfg_b, bo, bn, sinks, False))
    res = _sample_attend(qkv_s, group_args)
    outs_s, lses_s = [res[0], res[3], res[6]], [res[1], res[4], res[7]]
    new_s = [_position_major(res[3 * g + 2], caches_a[g].shape) for g in range(N_A_GROUPS)]
    out_b_s = res[9]
    new_b_sample = _position_major(res[10], cache_b_kv.shape)
    h_s, hnt_s = _finish(xs, outs_s, lses_s, out_b_s, gates_s, wa_bf, wb_bf, wo_bf, norm2)
    y_sample = _peer(hnt_s, h_s, wq_t, sk1, sk2, u_bf, vt_bf).reshape(dec_batch, dec_seq, D_MODEL)

    return (y_prompt, y_sample, new_p[0], new_p[1], new_p[2], new_b_prompt,
            new_s[0], new_s[1], new_s[2], new_b_sample)
```

```python
import functools
import math

import jax
import jax.numpy as jnp
from jax import lax
from jax.experimental import pallas as pl
from jax.experimental.pallas import tpu as pltpu

f32 = jnp.float32
bf16 = jnp.bfloat16

D_MODEL = 1024
HEAD_DIM = 64
A_SPAN = 128
A_DILATIONS = (1, 4, 16)
A_HEADS = 4
N_A_GROUPS = 3
B_WINDOW = 128
B_Q_HEADS = 8
B_KV_HEADS = 2
N_BUCKETS = 32
MAX_DISTANCE = 2048
BLK = 128
DEC_SEQ = 8
A_GROUP_COLS = 3 * A_HEADS * HEAD_DIM
A_COLS = N_A_GROUPS * A_GROUP_COLS
B_Q_COLS = B_Q_HEADS * HEAD_DIM
B_KV_COLS = B_KV_HEADS * HEAD_DIM
QKV_COLS = A_COLS + B_Q_COLS + 2 * B_KV_COLS
GATE_COLS = 2 * D_MODEL
N_SUBKEYS = 128
N_EXPERTS = N_SUBKEYS * N_SUBKEYS
PEER_HEADS = 8
PEER_TOPK = 16
PEER_HALF = 128
EPS = 1e-6
NEG = -1e30
SCALE = HEAD_DIM ** -0.5

LANES = 128
SUBLANES = 8
MXU_COLS = 256
VMEM_LIMIT = 56 * 1024 * 1024

PROJ_TM = 256
PEER_T = 512
PEER_EB = 2048
PEER_SUB = 512
assert (PEER_EB // N_SUBKEYS) % SUBLANES == 0 and PEER_EB % PEER_SUB == 0 and PEER_SUB % N_SUBKEYS == 0


def _const_spec(shape):
    nd = len(shape)
    return pl.BlockSpec(shape, lambda *_: (0,) * nd, pipeline_mode=pl.Buffered(1))


def _params(sem):
    return pltpu.CompilerParams(dimension_semantics=sem, vmem_limit_bytes=VMEM_LIMIT)


_NORM_ALL, _NORM_NONE, _NORM_FIRST_HALF = 0, 1, 2


def _chunk_norm_mode(c):
    col = c * MXU_COLS
    if col < A_COLS:
        return _NORM_NONE if (col % A_GROUP_COLS) == 2 * A_HEADS * HEAD_DIM else _NORM_ALL
    if col < A_COLS + B_Q_COLS:
        return _NORM_ALL
    return _NORM_FIRST_HALF


def _proj_kernel(x_ref, g1_ref, w_ref, gain_ref, seg_ref, qkv_ref, gate_ref):
    x = x_ref[...]
    ms = jnp.mean(x * x, axis=-1, keepdims=True)
    xn = ((x * lax.rsqrt(ms + EPS)) * g1_ref[...]).astype(bf16)
    seg = seg_ref[...]
    for c in range(QKV_COLS // MXU_COLS):
        cols = slice(c * MXU_COLS, (c + 1) * MXU_COLS)
        hb = jnp.dot(xn, w_ref[:, cols], preferred_element_type=f32)
        mode = _chunk_norm_mode(c)
        if mode != _NORM_NONE:
            h2 = hb * hb
            hi = h2.astype(bf16)
            lo = (h2 - hi.astype(f32)).astype(bf16)
            ss = (jnp.dot(hi, seg, preferred_element_type=f32)
                  + jnp.dot(lo, seg, preferred_element_type=f32))
            normed = (hb * lax.rsqrt(ss * (1.0 / HEAD_DIM) + EPS)) * gain_ref[:, cols]
            if mode == _NORM_ALL:
                hb = normed
            else:
                hb = jnp.concatenate([normed[:, :LANES], hb[:, LANES:]], axis=1)
        qkv_ref[:, cols] = hb
    hg = jnp.dot(xn, w_ref[:, QKV_COLS:], preferred_element_type=f32)
    gate_ref[...] = jax.nn.sigmoid(hg)


def _project(x, norm1, w_in_bf, gain, seg):
    n = x.shape[0]
    tm = PROJ_TM
    assert n % tm == 0
    return pl.pallas_call(
        _proj_kernel,
        out_shape=(jax.ShapeDtypeStruct((n, QKV_COLS), f32), jax.ShapeDtypeStruct((n, GATE_COLS), f32)),
        grid=(n // tm,),
        in_specs=[
            pl.BlockSpec((tm, D_MODEL), lambda i: (i, 0)),
            _const_spec((1, D_MODEL)),
            _const_spec((D_MODEL, QKV_COLS + GATE_COLS)),
            _const_spec((1, QKV_COLS)),
            _const_spec((MXU_COLS, MXU_COLS)),
        ],
        out_specs=(pl.BlockSpec((tm, QKV_COLS), lambda i: (i, 0)),
                   pl.BlockSpec((tm, GATE_COLS), lambda i: (i, 0))),
        compiler_params=_params(("parallel",)),
        name="project",
    )(x, norm1.reshape(1, D_MODEL), w_in_bf, gain, seg)


def _t5_bucket(dist):
    exact = N_BUCKETS // 2
    d32 = jnp.maximum(dist, 1).astype(f32)
    large = exact + (jnp.log(d32 / exact) / math.log(MAX_DISTANCE / exact) * (N_BUCKETS - exact)).astype(jnp.int32)
    large = jnp.minimum(large, N_BUCKETS - 1)
    return jnp.where(dist < exact, dist, large)


def _toeplitz(period_vals, rows, width):
    reps = -(-rows * width // (width + 1))
    return jnp.tile(period_vals, (1, reps))[:, :rows * width].reshape(-1, rows, width)


def _band_bias(table_h, d, span):
    k = jnp.arange(2 * BLK + 1)
    steps = BLK - k
    vals = table_h[_t5_bucket(jnp.maximum(steps, 0) * d)].astype(f32).T
    vals = jnp.where(((steps >= 0) & (steps <= span))[None], vals, NEG)
    return _toeplitz(vals, BLK, 2 * BLK)


def _sample_bias(table_h, d, steps, buf_len):
    assert buf_len == steps * d
    width = buf_len + DEC_SEQ
    dist = buf_len - jnp.arange(width + 1)
    ok = (dist >= 0) & (dist % d == 0)
    vals = table_h[_t5_bucket(jnp.maximum(dist, 0))].astype(f32).T
    vals = jnp.where(ok[None], vals, NEG)
    b = _toeplitz(vals, DEC_SEQ, width).reshape(-1, width)
    old = b[:, :buf_len]
    new = jnp.pad(b[:, buf_len:], ((0, 0), (0, LANES - DEC_SEQ)), constant_values=NEG)
    return old, new


def _band_kernel(*refs, n_q, n_kv, head_cfg, has_sink, want_lse, dilation):
    q_refs = refs[:n_q]
    pos = n_q
    kp_refs, kc_refs, vp_refs, vc_refs = (refs[pos + i * n_kv:pos + (i + 1) * n_kv] for i in range(4))
    pos += 4 * n_kv
    bias_ref = refs[pos]
    pos += 1
    sink_ref = None
    if has_sink:
        sink_ref = refs[pos]
        pos += 1
    o_ref = refs[pos]
    pos += 1
    l_ref = None
    if want_lse:
        l_ref = refs[pos]
        pos += 1
    o_stage = refs[pos:pos + n_q] if dilation > 1 else None
    l_stage = refs[pos + n_q:pos + 2 * n_q] if dilation > 1 and want_lse else None

    blk = pl.program_id(0)
    col = lax.broadcasted_iota(jnp.int32, (BLK, 2 * BLK), 1)
    no_prev = jnp.where((col < BLK) & (blk == 0), NEG, 0.0).astype(f32)
    contract_last = (((1,), (1,)), ((), ()))
    heads_per_ref = LANES // HEAD_DIM

    def residue(r, carry):
        rows = pl.ds(r, BLK, stride=dilation) if dilation > 1 else slice(None)
        qs = [ref[rows, :] for ref in q_refs]
        kps, kcs = [ref[rows, :] for ref in kp_refs], [ref[rows, :] for ref in kc_refs]
        vps, vcs = [ref[rows, :] for ref in vp_refs], [ref[rows, :] for ref in vc_refs]
        outs, lses = [], []
        for h, (qi, qoff, ki, koff) in enumerate(head_cfg):
            q = qs[qi][:, qoff:qoff + HEAD_DIM].astype(bf16)
            k = jnp.concatenate([kps[ki][:, koff:koff + HEAD_DIM], kcs[ki][:, koff:koff + HEAD_DIM]],
                                axis=0).astype(bf16)
            v = jnp.concatenate([vps[ki][:, koff:koff + HEAD_DIM], vcs[ki][:, koff:koff + HEAD_DIM]],
                                axis=0).astype(bf16)
            s = lax.dot_general(q, k, contract_last, preferred_element_type=f32) * SCALE + bias_ref[h] + no_prev
            m = jnp.max(s, axis=-1, keepdims=True)
            if has_sink:
                sink = sink_ref[h]
                m = jnp.maximum(m, sink)
            e = jnp.exp(s - m)
            den = jnp.sum(e, axis=-1, keepdims=True)
            if has_sink:
                den = den + jnp.exp(sink - m)
            p = (e / den).astype(bf16)
            outs.append(jnp.dot(p, v, preferred_element_type=f32))
            if want_lse:
                lses.append(jnp.broadcast_to(m + jnp.log(den), (BLK, HEAD_DIM)))
        for j in range(n_q):
            o = jnp.concatenate(outs[j * heads_per_ref:(j + 1) * heads_per_ref], axis=1)
            if dilation > 1:
                o_stage[j][rows, :] = o
            else:
                o_ref[:, j * LANES:(j + 1) * LANES] = o
            if want_lse:
                lse = jnp.concatenate(lses[j * heads_per_ref:(j + 1) * heads_per_ref], axis=1)
                if dilation > 1:
                    l_stage[j][rows, :] = lse
                else:
                    l_ref[:, j * LANES:(j + 1) * LANES] = lse
        return carry

    if dilation > 1:
        lax.fori_loop(0, dilation, residue, 0)
        for j in range(n_q):
            o_ref[:, j * LANES:(j + 1) * LANES] = o_stage[j][...]
            if want_lse:
                l_ref[:, j * LANES:(j + 1) * LANES] = l_stage[j][...]
    else:
        residue(0, 0)


def _band_attend_a(qkv, g, d, bias):
    s_len = qkv.shape[0]
    rows = d * BLK
    assert s_len % rows == 0
    width = A_HEADS * HEAD_DIM
    n_ref = width // LANES
    base = g * 3 * n_ref

    def specs(part, prev):
        if prev:
            return [pl.BlockSpec((rows, LANES), lambda b, c=base + part * n_ref + j: (jnp.maximum(b - 1, 0), c))
                    for j in range(n_ref)]
        return [pl.BlockSpec((rows, LANES), lambda b, c=base + part * n_ref + j: (b, c)) for j in range(n_ref)]

    heads_per_ref = LANES // HEAD_DIM
    head_cfg = tuple((h // heads_per_ref, (h % heads_per_ref) * HEAD_DIM) * 2 for h in range(A_HEADS))
    in_specs = specs(0, False) + specs(1, True) + specs(1, False) + specs(2, True) + specs(2, False)
    scratch = [pltpu.VMEM((rows, LANES), f32)] * (2 * n_ref) if d > 1 else []
    return pl.pallas_call(
        functools.partial(_band_kernel, n_q=n_ref, n_kv=n_ref, head_cfg=head_cfg, has_sink=False, want_lse=True,
                          dilation=d),
        out_shape=(jax.ShapeDtypeStruct((s_len, width), f32), jax.ShapeDtypeStruct((s_len, width), f32)),
        grid=(s_len // rows,),
        in_specs=in_specs + [_const_spec((A_HEADS, BLK, 2 * BLK))],
        out_specs=(pl.BlockSpec((rows, width), lambda b: (b, 0)),
                   pl.BlockSpec((rows, width), lambda b: (b, 0))),
        scratch_shapes=scratch,
        compiler_params=_params(("parallel",)),
        name=f"band_a{g}",
    )(*([qkv] * len(in_specs)), bias)


def _band_attend_b(qkv, bias, sinks):
    s_len = qkv.shape[0]
    assert s_len % BLK == 0
    assert B_KV_COLS == LANES
    n_q = B_Q_COLS // LANES
    q_blk = A_COLS // LANES
    k_blk = (A_COLS + B_Q_COLS) // LANES
    heads_per_ref = LANES // HEAD_DIM
    group = B_Q_HEADS // B_KV_HEADS
    head_cfg = tuple((h // heads_per_ref, (h % heads_per_ref) * HEAD_DIM, 0, (h // group) * HEAD_DIM)
                     for h in range(B_Q_HEADS))

    def kv_spec(off, prev):
        if prev:
            return pl.BlockSpec((BLK, LANES), lambda b: (jnp.maximum(b - 1, 0), k_blk + off))
        return pl.BlockSpec((BLK, LANES), lambda b: (b, k_blk + off))

    in_specs = [pl.BlockSpec((BLK, LANES), lambda b, c=q_blk + j: (b, c)) for j in range(n_q)]
    in_specs += [kv_spec(0, True), kv_spec(0, False), kv_spec(1, True), kv_spec(1, False)]
    return pl.pallas_call(
        functools.partial(_band_kernel, n_q=n_q, n_kv=1, head_cfg=head_cfg, has_sink=True, want_lse=False,
                          dilation=1),
        out_shape=jax.ShapeDtypeStruct((s_len, B_Q_COLS), f32),
        grid=(s_len // BLK,),
        in_specs=in_specs + [_const_spec((B_Q_HEADS, BLK, 2 * BLK)), pl.BlockSpec(memory_space=pltpu.SMEM)],
        out_specs=pl.BlockSpec((BLK, B_Q_COLS), lambda b: (b, 0)),
        compiler_params=_params(("parallel",)),
        name="band_b",
    )(*([qkv] * len(in_specs)), bias, sinks)


def _sample_kernel(*refs, groups):
    n_in = sum(n_q + 5 + int(has_sink) for (n_q, _, has_sink, _, _) in groups)
    ins, outs = list(refs[:n_in]), list(refs[n_in:])
    for (n_q, head_cfg, has_sink, want_lse, kv_cols) in groups:
        q_refs = [ins.pop(0) for _ in range(n_q)]
        kn_ref, vn_ref, buf_ref, bo_ref, bn_ref = (ins.pop(0) for _ in range(5))
        sink_ref = ins.pop(0) if has_sink else None
        o_ref = outs.pop(0)
        l_ref = outs.pop(0) if want_lse else None
        nb_ref = outs.pop(0)
        _sample_group(q_refs, kn_ref, vn_ref, buf_ref, bo_ref, bn_ref, sink_ref, o_ref, l_ref, nb_ref,
                      head_cfg, kv_cols)


def _sample_group(q_refs, kn_ref, vn_ref, buf_ref, bo_ref, bn_ref, sink_ref, o_ref, l_ref, nb_ref, head_cfg, kv_cols):
    has_sink = sink_ref is not None
    want_lse = l_ref is not None

    buf_len = buf_ref.shape[2]
    t_new = DEC_SEQ
    k_new = kn_ref[...]
    v_new = vn_ref[...]
    pad = jnp.zeros((LANES - t_new, kv_cols), f32)
    k_pad = jnp.concatenate([k_new, pad], axis=0)
    v_pad = jnp.concatenate([v_new, pad], axis=0)
    buf = buf_ref[0]

    new_t = jnp.concatenate([k_pad, v_pad], axis=1).T
    shifted = pltpu.roll(buf, buf_len - t_new, 1)
    lane = lax.broadcasted_iota(jnp.int32, (2 * kv_cols, LANES), 1)
    tail = jnp.where(lane >= LANES - t_new, pltpu.roll(new_t, LANES - t_new, 1), shifted[:, buf_len - LANES:])
    if buf_len > LANES:
        nb_ref[0, :, 0:buf_len - LANES] = shifted[:, 0:buf_len - LANES]
    nb_ref[0, :, buf_len - LANES:buf_len] = tail

    rows = []
    for (qi, qoff, kvoff) in head_cfg:
        q = q_refs[qi][:, qoff:qoff + HEAD_DIM]
        pieces = []
        if kvoff > 0:
            pieces.append(jnp.zeros((t_new, kvoff), f32))
        pieces.append(q)
        if kv_cols - kvoff - HEAD_DIM > 0:
            pieces.append(jnp.zeros((t_new, kv_cols - kvoff - HEAD_DIM), f32))
        rows.append(jnp.concatenate(pieces, axis=1) if len(pieces) > 1 else q)
    qrows = jnp.concatenate(rows, axis=0).astype(bf16)

    kt_old = buf[0:kv_cols, :].astype(bf16)
    vt_old = buf[kv_cols:2 * kv_cols, :].astype(bf16)
    k_pad = k_pad.astype(bf16)
    v_pad = v_pad.astype(bf16)

    contract_last = (((1,), (1,)), ((), ()))
    s_old = jnp.dot(qrows, kt_old, preferred_element_type=f32) * SCALE + bo_ref[...]
    s_new = lax.dot_general(qrows, k_pad, contract_last, preferred_element_type=f32) * SCALE + bn_ref[...]
    m = jnp.maximum(jnp.max(s_old, axis=-1, keepdims=True), jnp.max(s_new, axis=-1, keepdims=True))
    n_heads = len(head_cfg)
    sink_col = None
    if has_sink:
        sink_col = jnp.concatenate(
            [jnp.full((t_new, 1), sink_ref[h], f32) for h in range(n_heads)], axis=0)
        m = jnp.maximum(m, sink_col)
    e_old = jnp.exp(s_old - m)
    e_new = jnp.exp(s_new - m)
    den = jnp.sum(e_old, axis=-1, keepdims=True) + jnp.sum(e_new, axis=-1, keepdims=True)
    if has_sink:
        den = den + jnp.exp(sink_col - m)
    acc = (lax.dot_general((e_old / den).astype(bf16), vt_old, contract_last, preferred_element_type=f32)
           + jnp.dot((e_new / den).astype(bf16), v_pad, preferred_element_type=f32))
    lse = m + jnp.log(den)
    for h, (qi, qoff, kvoff) in enumerate(head_cfg):
        o_ref[:, h * HEAD_DIM:(h + 1) * HEAD_DIM] = acc[h * t_new:(h + 1) * t_new, kvoff:kvoff + HEAD_DIM]
        if want_lse:
            l_ref[:, h * HEAD_DIM:(h + 1) * HEAD_DIM] = jnp.broadcast_to(
                lse[h * t_new:(h + 1) * t_new, :], (t_new, HEAD_DIM))


def _position_minor(buf):
    bd, buf_len = buf.shape[:2]
    return jnp.transpose(buf, (0, 2, 3, 4, 1)).reshape(bd, -1, buf_len)


def _position_major(buf_t, shape):
    bd, buf_len, two, groups, hd = shape
    return jnp.transpose(buf_t.reshape(bd, two, groups, hd, buf_len), (0, 4, 1, 2, 3))


def _sample_attend(qkv, group_args):
    bd = group_args[0][0].shape[0]
    assert qkv.shape[0] == bd * DEC_SEQ

    def col_spec(width, idx):
        return pl.BlockSpec((DEC_SEQ, width), lambda b: (b, idx))

    in_specs, args, out_shape, out_specs, groups = [], [], [], [], []
    for (buf, q_blocks, k_block, v_block, kv_cols, head_cfg, bias_old, bias_new, sinks, want_lse) in group_args:
        _, width2, buf_len = buf.shape
        assert width2 == 2 * kv_cols and buf.shape[0] == bd and buf_len % LANES == 0
        rows = len(head_cfg) * DEC_SEQ
        in_specs += [col_spec(w, i) for (w, i) in q_blocks]
        in_specs += [col_spec(*k_block), col_spec(*v_block),
                     pl.BlockSpec((1, width2, buf_len), lambda b: (b, 0, 0)),
                     _const_spec((rows, buf_len)), _const_spec((rows, LANES))]
        args += [qkv] * (len(q_blocks) + 2) + [buf, bias_old, bias_new]
        if sinks is not None:
            in_specs.append(pl.BlockSpec(memory_space=pltpu.SMEM))
            args.append(sinks)
        out_w = len(head_cfg) * HEAD_DIM
        for _ in range(2 if want_lse else 1):
            out_shape.append(jax.ShapeDtypeStruct((bd * DEC_SEQ, out_w), f32))
            out_specs.append(pl.BlockSpec((DEC_SEQ, out_w), lambda b: (b, 0)))
        out_shape.append(jax.ShapeDtypeStruct(buf.shape, f32))
        out_specs.append(pl.BlockSpec((1, width2, buf_len), lambda b: (b, 0, 0)))
        groups.append((len(q_blocks), head_cfg, sinks is not None, want_lse, kv_cols))
    return pl.pallas_call(
        functools.partial(_sample_kernel, groups=tuple(groups)),
        out_shape=tuple(out_shape),
        grid=(bd,),
        in_specs=in_specs,
        out_specs=tuple(out_specs),
        compiler_params=_params(("parallel",)),
        name="sample",
    )(*args)


def _finish_kernel(x_ref, oa0_ref, oa1_ref, oa2_ref, la0_ref, la1_ref, la2_ref, ob_ref, gate_ref,
                   wa_ref, wb_ref, wo_ref, g2_ref, h_ref, hnt_ref):
    l0, l1, l2 = la0_ref[...], la1_ref[...], la2_ref[...]
    m = jnp.maximum(jnp.maximum(l0, l1), l2)
    e0, e1, e2 = jnp.exp(l0 - m), jnp.exp(l1 - m), jnp.exp(l2 - m)
    den = e0 + e1 + e2
    oa = (e0 / den) * oa0_ref[...] + (e1 / den) * oa1_ref[...] + (e2 / den) * oa2_ref[...]
    ya = jnp.dot(oa.astype(bf16), wa_ref[...], preferred_element_type=f32)
    yb = jnp.dot(ob_ref[...].astype(bf16), wb_ref[...], preferred_element_type=f32)
    merged = gate_ref[:, :D_MODEL] * ya + gate_ref[:, D_MODEL:] * yb
    h = x_ref[...] + jnp.dot(merged.astype(bf16), wo_ref[...], preferred_element_type=f32)
    h_ref[...] = h
    ms = jnp.mean(h * h, axis=-1, keepdims=True)
    hn = (h * lax.rsqrt(ms + EPS)) * g2_ref[...]
    hnt_ref[...] = hn.T.astype(bf16)


def _finish(x, outs_a, lses_a, out_b, gates, wa_bf, wb_bf, wo_bf, norm2):
    n = x.shape[0]
    tm = PROJ_TM
    assert n % tm == 0
    aw = A_HEADS * HEAD_DIM

    def row_spec(w):
        return pl.BlockSpec((tm, w), lambda i: (i, 0))

    return pl.pallas_call(
        _finish_kernel,
        out_shape=(jax.ShapeDtypeStruct((n, D_MODEL), f32), jax.ShapeDtypeStruct((D_MODEL, n), bf16)),
        grid=(n // tm,),
        in_specs=[row_spec(D_MODEL)] + [row_spec(aw)] * 6 + [row_spec(B_Q_COLS), row_spec(GATE_COLS),
                  _const_spec((aw, D_MODEL)), _const_spec((B_Q_COLS, D_MODEL)), _const_spec((D_MODEL, D_MODEL)),
                  _const_spec((1, D_MODEL))],
        out_specs=(row_spec(D_MODEL), pl.BlockSpec((D_MODEL, tm), lambda i: (0, i))),
        compiler_params=_params(("parallel",)),
        name="finish",
    )(x, *outs_a, *lses_a, out_b, gates, wa_bf, wb_bf, wo_bf, norm2.reshape(1, D_MODEL))


def _batcher_pairs(n):
    pairs = []
    p = 1
    while p < n:
        k = p
        while k >= 1:
            for j in range(k % p, n - k, 2 * k):
                for i in range(min(k, n - j - k)):
                    if (i + j) // (2 * p) == (i + j + k) // (2 * p):
                        pairs.append((i + j, i + j + k))
            k //= 2
        p *= 2
    return pairs


_SORT16 = _batcher_pairs(PEER_TOPK)
_BITONIC16 = [(i, i + dist) for dist in (8, 4, 2, 1) for i in range(PEER_TOPK) if not i & dist]


def _exchange(xs, i, j):
    a, b = xs[i], xs[j]
    if b is None:
        return
    if a is None:
        xs[i], xs[j] = b, None
        return
    xs[i], xs[j] = jnp.maximum(a, b), jnp.minimum(a, b)


def _sort16_desc(xs):
    xs = list(xs)
    for i, j in _SORT16:
        _exchange(xs, i, j)
    return xs


def _merge_top16(xs, ys):
    zs = []
    for j in range(PEER_TOPK):
        a, b = xs[j], ys[PEER_TOPK - 1 - j]
        zs.append(b if a is None else a if b is None else jnp.maximum(a, b))
    for i, j in _BITONIC16:
        _exchange(zs, i, j)
    return zs


_CAND_PAIRS = [(a, b) for a in range(PEER_TOPK) for b in range(PEER_TOPK) if (a + 1) * (b + 1) <= PEER_TOPK]


def _peer_route(hnt_ref, wq_ref, sk1_ref, sk2_ref, n1_ref, s2_ref, e1_ref, r2_ref, e2_ref, top1_ref, top2_ref):
    hnt = hnt_ref[...]
    for half, (sk_ref, s_ref, top_ref) in enumerate(((sk1_ref, n1_ref, top1_ref), (sk2_ref, s2_ref, top2_ref))):
        for h in range(PEER_HEADS):
            r0 = (2 * h + half) * PEER_HALF
            qh = jnp.dot(wq_ref[r0:r0 + PEER_HALF, :], hnt, preferred_element_type=f32).astype(bf16)
            st = jnp.dot(sk_ref[...], qh, preferred_element_type=f32)
            s_ref[h] = st
            xs = _sort16_desc([st[j * SUBLANES:(j + 1) * SUBLANES, :] for j in range(N_SUBKEYS // SUBLANES)])
            for shift in (4, 2, 1):
                xs = _merge_top16(xs, [pltpu.roll(x, shift, 0) for x in xs])
            for a in range(PEER_TOPK):
                top_ref[a, h:h + 1, :] = xs[a][0:1, :]
    v1 = [top1_ref[a] for a in range(PEER_TOPK)]
    v2 = [top2_ref[a] for a in range(PEER_TOPK)]
    cands = [v1[a] + v2[b] for a, b in _CAND_PAIRS]
    cands += [None] * (-len(cands) % PEER_TOPK)
    groups = [_sort16_desc(cands[i::len(cands) // PEER_TOPK]) for i in range(len(cands) // PEER_TOPK)]
    while len(groups) > 1:
        groups = [_merge_top16(groups[i], groups[i + 1]) for i in range(0, len(groups), 2)]
    sc = groups[0]
    den = jnp.zeros_like(sc[0])
    for k in range(PEER_TOPK):
        den = den + jnp.exp(sc[k] - sc[0])
    inv_den = 1.0 / den
    tau = sc[PEER_TOPK - 1]
    tile = 2 * SUBLANES

    def count_tile(j, carry):
        rows = pl.ds(pl.multiple_of(j * tile, tile), tile)
        for h in range(PEER_HEADS):
            head = slice(h, h + 1)
            s1, s2 = n1_ref[h, rows, :], s2_ref[h, rows, :]
            e1_ref[h, rows, :] = jnp.exp(s1 - top1_ref[0, head, :]) * inv_den[head, :]
            e2_ref[h, rows, :] = jnp.exp(s2 - top2_ref[0, head, :]).astype(bf16)
            reach = jnp.full_like(s1, float(PEER_TOPK))
            rank = jnp.full_like(s2, float(PEER_TOPK))
            for b in reversed(range(PEER_TOPK)):
                best = top2_ref[b, head, :]
                reach = jnp.where(s1 + best < tau[head, :], float(b), reach)
                rank = jnp.where(best <= s2, float(b), rank)
            n1_ref[h, rows, :] = reach
            r2_ref[h, rows, :] = rank.astype(bf16)
        return carry

    lax.fori_loop(0, N_SUBKEYS // tile, count_tile, 0)


def _gelu_exact(x):
    return 0.5 * x * (1.0 + lax.erf(x * (2.0 ** -0.5)))


def _peer_kernel(hnt_ref, h_ref, wq_ref, sk1_ref, sk2_ref, u_ref, vt_ref, y_ref,
                 n1_ref, s2_ref, e1_ref, r2_ref, e2_ref, top1_ref, top2_ref, hu0_ref, hu1_ref, act0_ref, act1_ref,
                 acc_ref):
    eb = pl.program_id(1)
    n_sub = PEER_EB // PEER_SUB
    blocks = PEER_SUB // N_SUBKEYS

    @pl.when(eb == 0)
    def _():
        _peer_route(hnt_ref, wq_ref, sk1_ref, sk2_ref, n1_ref, s2_ref, e1_ref, r2_ref, e2_ref, top1_ref, top2_ref)
        acc_ref[...] = jnp.zeros_like(acc_ref)

    hu_refs, act_refs = (hu0_ref, hu1_ref), (act0_ref, act1_ref)

    def project_sub(k, slot):
        hu_refs[slot][...] = jnp.dot(u_ref[k], hnt_ref[...], preferred_element_type=f32)

    def activate_sub(k, slot):
        for i in range(blocks):
            local = k * blocks + i
            base = eb * (PEER_EB // N_SUBKEYS) + (local // SUBLANES) * SUBLANES
            group = pl.ds(pl.multiple_of(base, SUBLANES), SUBLANES)
            to_top = (SUBLANES - local % SUBLANES) % SUBLANES
            rows = slice(i * N_SUBKEYS, (i + 1) * N_SUBKEYS)
            for c in range(PEER_T // LANES):
                tok = slice(c * LANES, (c + 1) * LANES)
                g = jnp.zeros((N_SUBKEYS, LANES), bf16)
                for h in range(PEER_HEADS):
                    n1_row = pltpu.roll(n1_ref[h, group, tok], to_top, 0)[0:1, :].astype(bf16)
                    e1_row = pltpu.roll(e1_ref[h, group, tok], to_top, 0)[0:1, :].astype(bf16)
                    w = e1_row * e2_ref[h, :, tok]
                    g = g + jnp.where(r2_ref[h, :, tok] < n1_row, w, jnp.zeros_like(w))
                act = g.astype(f32) * _gelu_exact(hu_refs[slot][rows, tok])
                act_refs[slot][rows, tok] = act.astype(bf16)

    def combine_sub(k, slot):
        acc_ref[...] += jnp.dot(vt_ref[k], act_refs[slot][...], preferred_element_type=f32)

    project_sub(0, 0)
    for k in range(n_sub):
        if k + 1 < n_sub:
            project_sub(k + 1, (k + 1) % 2)
        activate_sub(k, k % 2)
        if k >= 1:
            combine_sub(k - 1, (k - 1) % 2)
    combine_sub(n_sub - 1, (n_sub - 1) % 2)

    @pl.when(eb == pl.num_programs(1) - 1)
    def _():
        y_ref[...] = h_ref[...] + acc_ref[...].T


def _peer(hnt, h, wq_t, sk1, sk2, u_bf, vt_bf):
    n = h.shape[0]
    t = PEER_T
    assert n % t == 0
    return pl.pallas_call(
        _peer_kernel,
        out_shape=jax.ShapeDtypeStruct((n, D_MODEL), f32),
        grid=(n // t, N_EXPERTS // PEER_EB),
        in_specs=[pl.BlockSpec((D_MODEL, t), lambda j, e: (0, j)),
                  pl.BlockSpec((t, D_MODEL), lambda j, e: (j, 0)),
                  _const_spec((2 * PEER_HEADS * PEER_HALF, D_MODEL)),
                  _const_spec((N_SUBKEYS, PEER_HALF)), _const_spec((N_SUBKEYS, PEER_HALF)),
                  pl.BlockSpec((PEER_EB // PEER_SUB, PEER_SUB, D_MODEL), lambda j, e: (e, 0, 0)),
                  pl.BlockSpec((PEER_EB // PEER_SUB, D_MODEL, PEER_SUB), lambda j, e: (e, 0, 0))],
        out_specs=pl.BlockSpec((t, D_MODEL), lambda j, e: (j, 0)),
        scratch_shapes=[pltpu.VMEM((PEER_HEADS, N_SUBKEYS, t), f32)] * 3 + [
            pltpu.VMEM((PEER_HEADS, N_SUBKEYS, t), bf16),
            pltpu.VMEM((PEER_HEADS, N_SUBKEYS, t), bf16),
            pltpu.VMEM((PEER_TOPK, PEER_HEADS, t), f32),
            pltpu.VMEM((PEER_TOPK, PEER_HEADS, t), f32),
            pltpu.VMEM((PEER_SUB, t), f32), pltpu.VMEM((PEER_SUB, t), f32),
            pltpu.VMEM((PEER_SUB, t), bf16), pltpu.VMEM((PEER_SUB, t), bf16),
            pltpu.VMEM((D_MODEL, t), f32)],
        compiler_params=_params(("parallel", "arbitrary")),
        name="peer",
    )(hnt, h, wq_t, sk1, sk2, u_bf, vt_bf)


def kernel(x_prompt, x_sample, cache_a1_kv, cache_a2_kv, cache_a3_kv, cache_b_kv, norm1, w_in, q_norm_a, k_norm_a,
           q_norm_b, k_norm_b, rel_bias_table, sinks_b, w_up_a, w_up_b, w_out, norm2, w_peer_q, sub_keys_1,
           sub_keys_2, expert_u, expert_v):
    batch, seq, _ = x_prompt.shape
    dec_batch, dec_seq, _ = x_sample.shape
    assert batch == 1 and dec_seq == DEC_SEQ

    w_in_bf = w_in.astype(bf16)
    wa_bf, wb_bf, wo_bf = w_up_a.astype(bf16), w_up_b.astype(bf16), w_out.astype(bf16)
    wq_t = w_peer_q.T.astype(bf16)
    sk1, sk2 = sub_keys_1.astype(bf16), sub_keys_2.astype(bf16)
    u_bf = expert_u.astype(bf16).reshape(N_EXPERTS // PEER_SUB, PEER_SUB, D_MODEL)
    vt_bf = jnp.transpose(expert_v.reshape(N_EXPERTS // PEER_SUB, PEER_SUB, D_MODEL), (0, 2, 1)).astype(bf16)
    ones_v = jnp.ones((A_HEADS * HEAD_DIM,), f32)
    gain = jnp.concatenate(
        [jnp.concatenate([jnp.tile(q_norm_a[g], A_HEADS), jnp.tile(k_norm_a[g], A_HEADS), ones_v])
         for g in range(N_A_GROUPS)]
        + [jnp.tile(q_norm_b, B_Q_HEADS), jnp.tile(k_norm_b, B_KV_HEADS), jnp.ones((B_KV_COLS,), f32)]
    ).astype(f32).reshape(1, QKV_COLS)
    lane_head = jnp.arange(MXU_COLS) // HEAD_DIM
    seg = (lane_head[:, None] == lane_head[None, :]).astype(bf16)
    b_heads = slice(N_A_GROUPS * A_HEADS, N_A_GROUPS * A_HEADS + B_Q_HEADS)
    tables_a = [rel_bias_table[:, g * A_HEADS:(g + 1) * A_HEADS] for g in range(N_A_GROUPS)]
    table_b = rel_bias_table[:, b_heads]
    sinks = sinks_b.astype(f32)

    xp = x_prompt.reshape(seq, D_MODEL)
    qkv_p, gates_p = _project(xp, norm1, w_in_bf, gain, seg)
    outs_p, lses_p = [], []
    for g, d in enumerate(A_DILATIONS):
        o, l = _band_attend_a(qkv_p, g, d, _band_bias(tables_a[g], d, A_SPAN))
        outs_p.append(o)
        lses_p.append(l)
    out_b_p = _band_attend_b(qkv_p, _band_bias(table_b, 1, B_WINDOW), sinks)
    h_p, hnt_p = _finish(xp, outs_p, lses_p, out_b_p, gates_p, wa_bf, wb_bf, wo_bf, norm2)
    y_prompt = _peer(hnt_p, h_p, wq_t, sk1, sk2, u_bf, vt_bf).reshape(batch, seq, D_MODEL)

    new_p = []
    for g, d in enumerate(A_DILATIONS):
        lp = min(A_SPAN * d, seq)
        c0 = g * A_GROUP_COLS + A_HEADS * HEAD_DIM
        new_p.append(qkv_p[seq - lp:, c0:c0 + 2 * A_HEADS * HEAD_DIM].reshape(batch, lp, 2, A_HEADS, HEAD_DIM))
    lpb = min(B_WINDOW, seq)
    new_b_prompt = qkv_p[seq - lpb:, A_COLS + B_Q_COLS:].reshape(batch, lpb, 2, B_KV_HEADS, HEAD_DIM)

    xs = x_sample.reshape(dec_batch * dec_seq, D_MODEL)
    qkv_s, gates_s = _project(xs, norm1, w_in_bf, gain, seg)
    aw = A_HEADS * HEAD_DIM
    caches_a = (cache_a1_kv, cache_a2_kv, cache_a3_kv)
    cfg_a = tuple((0, h * HEAD_DIM, h * HEAD_DIM) for h in range(A_HEADS))
    group_args = []
    for g, (d, buf) in enumerate(zip(A_DILATIONS, caches_a)):
        bo, bn = _sample_bias(tables_a[g], d, A_SPAN, buf.shape[1])
        group_args.append((_position_minor(buf), [(aw, 3 * g)], (aw, 3 * g + 1), (aw, 3 * g + 2), aw, cfg_a,
                           bo, bn, None, True))
    bo, bn = _sample_bias(table_b, 1, B_WINDOW, cache_b_kv.shape[1])
    cfg_b = tuple((h // 4, (h % 4) * HEAD_DIM, (h // 4) * HEAD_DIM) for h in range(B_Q_HEADS))
    qw = B_Q_COLS // 2
    k_blk = (A_COLS + B_Q_COLS) // B_KV_COLS
    group_args.append((_position_minor(cache_b_kv), [(qw, A_COLS // qw), (qw, A_COLS // qw + 1)],
                       (B_KV_COLS, k_blk), (B_KV_COLS, k_blk + 1), B_KV_COLS, cfg_b, bo, bn, sinks, False))
    res = _sample_attend(qkv_s, group_args)
    outs_s, lses_s = [res[0], res[3], res[6]], [res[1], res[4], res[7]]
    new_s = [_position_major(res[3 * g + 2], caches_a[g].shape) for g in range(N_A_GROUPS)]
    out_b_s = res[9]
    new_b_sample = _position_major(res[10], cache_b_kv.shape)
    h_s, hnt_s = _finish(xs, outs_s, lses_s, out_b_s, gates_s, wa_bf, wb_bf, wo_bf, norm2)
    y_sample = _peer(hnt_s, h_s, wq_t, sk1, sk2, u_bf, vt_bf).reshape(dec_batch, dec_seq, D_MODEL)

    return (y_prompt, y_sample, new_p[0], new_p[1], new_p[2], new_b_prompt,
            new_s[0], new_s[1], new_s[2], new_b_sample)
```

```python
import functools
import math

import jax
import jax.numpy as jnp
from jax import lax
from jax.experimental import pallas as pl
from jax.experimental.pallas import tpu as pltpu

f32 = jnp.float32
bf16 = jnp.bfloat16

D_MODEL = 1024
HEAD_DIM = 64
A_SPAN = 128
A_DILATIONS = (1, 4, 16)
A_HEADS = 4
N_A_GROUPS = 3
B_WINDOW = 128
B_Q_HEADS = 8
B_KV_HEADS = 2
N_BUCKETS = 32
MAX_DISTANCE = 2048
BLK = 128
DEC_SEQ = 8
A_GROUP_COLS = 3 * A_HEADS * HEAD_DIM
A_COLS = N_A_GROUPS * A_GROUP_COLS
B_Q_COLS = B_Q_HEADS * HEAD_DIM
B_KV_COLS = B_KV_HEADS * HEAD_DIM
QKV_COLS = A_COLS + B_Q_COLS + 2 * B_KV_COLS
GATE_COLS = 2 * D_MODEL
N_SUBKEYS = 128
N_EXPERTS = N_SUBKEYS * N_SUBKEYS
PEER_HEADS = 8
PEER_TOPK = 16
PEER_HALF = 128
EPS = 1e-6
NEG = -1e30
SCALE = HEAD_DIM ** -0.5

LANES = 128
SUBLANES = 8
MXU_COLS = 256
VMEM_LIMIT = 56 * 1024 * 1024

PROJ_TM = 512
PEER_T = 512
PEER_EB = 2048
PEER_SUB = 512
assert (PEER_EB // N_SUBKEYS) % SUBLANES == 0 and PEER_EB % PEER_SUB == 0 and PEER_SUB % N_SUBKEYS == 0


def _const_spec(shape):
    nd = len(shape)
    return pl.BlockSpec(shape, lambda *_: (0,) * nd, pipeline_mode=pl.Buffered(1))


def _params(sem):
    return pltpu.CompilerParams(dimension_semantics=sem, vmem_limit_bytes=VMEM_LIMIT)


_NORM_ALL, _NORM_NONE, _NORM_FIRST_HALF = 0, 1, 2


def _chunk_norm_mode(c):
    col = c * MXU_COLS
    if col < A_COLS:
        return _NORM_NONE if (col % A_GROUP_COLS) == 2 * A_HEADS * HEAD_DIM else _NORM_ALL
    if col < A_COLS + B_Q_COLS:
        return _NORM_ALL
    return _NORM_FIRST_HALF


def _proj_kernel(x_ref, g1_ref, w_ref, gain_ref, seg_ref, qkv_ref, gate_ref):
    x = x_ref[...]
    ms = jnp.mean(x * x, axis=-1, keepdims=True)
    xn = ((x * lax.rsqrt(ms + EPS)) * g1_ref[...]).astype(bf16)
    seg = seg_ref[...]
    for c in range(QKV_COLS // MXU_COLS):
        cols = slice(c * MXU_COLS, (c + 1) * MXU_COLS)
        hb = jnp.dot(xn, w_ref[:, cols], preferred_element_type=f32)
        mode = _chunk_norm_mode(c)
        if mode != _NORM_NONE:
            h2 = hb * hb
            hi = h2.astype(bf16)
            lo = (h2 - hi.astype(f32)).astype(bf16)
            ss = (jnp.dot(hi, seg, preferred_element_type=f32)
                  + jnp.dot(lo, seg, preferred_element_type=f32))
            normed = (hb * lax.rsqrt(ss * (1.0 / HEAD_DIM) + EPS)) * gain_ref[:, cols]
            if mode == _NORM_ALL:
                hb = normed
            else:
                hb = jnp.concatenate([normed[:, :LANES], hb[:, LANES:]], axis=1)
        qkv_ref[:, cols] = hb
    hg = jnp.dot(xn, w_ref[:, QKV_COLS:], preferred_element_type=f32)
    gate_ref[...] = jax.nn.sigmoid(hg)


def _project(x, norm1, w_in_bf, gain, seg):
    n = x.shape[0]
    tm = PROJ_TM
    assert n % tm == 0
    return pl.pallas_call(
        _proj_kernel,
        out_shape=(jax.ShapeDtypeStruct((n, QKV_COLS), f32), jax.ShapeDtypeStruct((n, GATE_COLS), f32)),
        grid=(n // tm,),
        in_specs=[
            pl.BlockSpec((tm, D_MODEL), lambda i: (i, 0)),
            _const_spec((1, D_MODEL)),
            _const_spec((D_MODEL, QKV_COLS + GATE_COLS)),
            _const_spec((1, QKV_COLS)),
            _const_spec((MXU_COLS, MXU_COLS)),
        ],
        out_specs=(pl.BlockSpec((tm, QKV_COLS), lambda i: (i, 0)),
                   pl.BlockSpec((tm, GATE_COLS), lambda i: (i, 0))),
        compiler_params=_params(("parallel",)),
        name="project",
    )(x, norm1.reshape(1, D_MODEL), w_in_bf, gain, seg)


def _t5_bucket(dist):
    exact = N_BUCKETS // 2
    d32 = jnp.maximum(dist, 1).astype(f32)
    large = exact + (jnp.log(d32 / exact) / math.log(MAX_DISTANCE / exact) * (N_BUCKETS - exact)).astype(jnp.int32)
    large = jnp.minimum(large, N_BUCKETS - 1)
    return jnp.where(dist < exact, dist, large)


def _toeplitz(period_vals, rows, width):
    reps = -(-rows * width // (width + 1))
    return jnp.tile(period_vals, (1, reps))[:, :rows * width].reshape(-1, rows, width)


def _band_bias(table_h, d, span):
    k = jnp.arange(2 * BLK + 1)
    steps = BLK - k
    vals = table_h[_t5_bucket(jnp.maximum(steps, 0) * d)].astype(f32).T
    vals = jnp.where(((steps >= 0) & (steps <= span))[None], vals, NEG)
    return _toeplitz(vals, BLK, 2 * BLK)


def _sample_bias(table_h, d, steps, buf_len):
    assert buf_len == steps * d
    width = buf_len + DEC_SEQ
    dist = buf_len - jnp.arange(width + 1)
    ok = (dist >= 0) & (dist % d == 0)
    vals = table_h[_t5_bucket(jnp.maximum(dist, 0))].astype(f32).T
    vals = jnp.where(ok[None], vals, NEG)
    b = _toeplitz(vals, DEC_SEQ, width).reshape(-1, width)
    old = b[:, :buf_len]
    new = jnp.pad(b[:, buf_len:], ((0, 0), (0, LANES - DEC_SEQ)), constant_values=NEG)
    return old, new


def _band_kernel(*refs, n_q, n_kv, head_cfg, has_sink, want_lse, dilation):
    q_refs = refs[:n_q]
    pos = n_q
    kp_refs, kc_refs, vp_refs, vc_refs = (refs[pos + i * n_kv:pos + (i + 1) * n_kv] for i in range(4))
    pos += 4 * n_kv
    bias_ref = refs[pos]
    pos += 1
    sink_ref = None
    if has_sink:
        sink_ref = refs[pos]
        pos += 1
    o_ref = refs[pos]
    pos += 1
    l_ref = None
    if want_lse:
        l_ref = refs[pos]
        pos += 1
    o_stage = refs[pos:pos + n_q] if dilation > 1 else None
    l_stage = refs[pos + n_q:pos + 2 * n_q] if dilation > 1 and want_lse else None

    blk = pl.program_id(0)
    col = lax.broadcasted_iota(jnp.int32, (BLK, 2 * BLK), 1)
    no_prev = jnp.where((col < BLK) & (blk == 0), NEG, 0.0).astype(f32)
    contract_last = (((1,), (1,)), ((), ()))
    heads_per_ref = LANES // HEAD_DIM

    def residue(r, carry):
        rows = pl.ds(r, BLK, stride=dilation) if dilation > 1 else slice(None)
        qs = [ref[rows, :] for ref in q_refs]
        kps, kcs = [ref[rows, :] for ref in kp_refs], [ref[rows, :] for ref in kc_refs]
        vps, vcs = [ref[rows, :] for ref in vp_refs], [ref[rows, :] for ref in vc_refs]
        outs, lses = [], []
        for h, (qi, qoff, ki, koff) in enumerate(head_cfg):
            q = qs[qi][:, qoff:qoff + HEAD_DIM].astype(bf16)
            k = jnp.concatenate([kps[ki][:, koff:koff + HEAD_DIM], kcs[ki][:, koff:koff + HEAD_DIM]],
                                axis=0).astype(bf16)
            v = jnp.concatenate([vps[ki][:, koff:koff + HEAD_DIM], vcs[ki][:, koff:koff + HEAD_DIM]],
                                axis=0).astype(bf16)
            s = lax.dot_general(q, k, contract_last, preferred_element_type=f32) * SCALE + bias_ref[h] + no_prev
            m = jnp.max(s, axis=-1, keepdims=True)
            if has_sink:
                sink = sink_ref[h]
                m = jnp.maximum(m, sink)
            e = jnp.exp(s - m)
            den = jnp.sum(e, axis=-1, keepdims=True)
            if has_sink:
                den = den + jnp.exp(sink - m)
            p = (e / den).astype(bf16)
            outs.append(jnp.dot(p, v, preferred_element_type=f32))
            if want_lse:
                lses.append(jnp.broadcast_to(m + jnp.log(den), (BLK, HEAD_DIM)))
        for j in range(n_q):
            o = jnp.concatenate(outs[j * heads_per_ref:(j + 1) * heads_per_ref], axis=1)
            if dilation > 1:
                o_stage[j][rows, :] = o
            else:
                o_ref[:, j * LANES:(j + 1) * LANES] = o
            if want_lse:
                lse = jnp.concatenate(lses[j * heads_per_ref:(j + 1) * heads_per_ref], axis=1)
                if dilation > 1:
                    l_stage[j][rows, :] = lse
                else:
                    l_ref[:, j * LANES:(j + 1) * LANES] = lse
        return carry

    if dilation > 1:
        lax.fori_loop(0, dilation, residue, 0)
        for j in range(n_q):
            o_ref[:, j * LANES:(j + 1) * LANES] = o_stage[j][...]
            if want_lse:
                l_ref[:, j * LANES:(j + 1) * LANES] = l_stage[j][...]
    else:
        residue(0, 0)


def _band_attend_a(qkv, g, d, bias):
    s_len = qkv.shape[0]
    rows = d * BLK
    assert s_len % rows == 0
    width = A_HEADS * HEAD_DIM
    n_ref = width // LANES
    base = g * 3 * n_ref

    def specs(part, prev):
        if prev:
            return [pl.BlockSpec((rows, LANES), lambda b, c=base + part * n_ref + j: (jnp.maximum(b - 1, 0), c))
                    for j in range(n_ref)]
        return [pl.BlockSpec((rows, LANES), lambda b, c=base + part * n_ref + j: (b, c)) for j in range(n_ref)]

    heads_per_ref = LANES // HEAD_DIM
    head_cfg = tuple((h // heads_per_ref, (h % heads_per_ref) * HEAD_DIM) * 2 for h in range(A_HEADS))
    in_specs = specs(0, False) + specs(1, True) + specs(1, False) + specs(2, True) + specs(2, False)
    scratch = [pltpu.VMEM((rows, LANES), f32)] * (2 * n_ref) if d > 1 else []
    return pl.pallas_call(
        functools.partial(_band_kernel, n_q=n_ref, n_kv=n_ref, head_cfg=head_cfg, has_sink=False, want_lse=True,
                          dilation=d),
        out_shape=(jax.ShapeDtypeStruct((s_len, width), f32), jax.ShapeDtypeStruct((s_len, width), f32)),
        grid=(s_len // rows,),
        in_specs=in_specs + [_const_spec((A_HEADS, BLK, 2 * BLK))],
        out_specs=(pl.BlockSpec((rows, width), lambda b: (b, 0)),
                   pl.BlockSpec((rows, width), lambda b: (b, 0))),
        scratch_shapes=scratch,
        compiler_params=_params(("parallel",)),
        name=f"band_a{g}",
    )(*([qkv] * len(in_specs)), bias)


def _band_attend_b(qkv, bias, sinks):
    s_len = qkv.shape[0]
    assert s_len % BLK == 0
    assert B_KV_COLS == LANES
    n_q = B_Q_COLS // LANES
    q_blk = A_COLS // LANES
    k_blk = (A_COLS + B_Q_COLS) // LANES
    heads_per_ref = LANES // HEAD_DIM
    group = B_Q_HEADS // B_KV_HEADS
    head_cfg = tuple((h // heads_per_ref, (h % heads_per_ref) * HEAD_DIM, 0, (h // group) * HEAD_DIM)
                     for h in range(B_Q_HEADS))

    def kv_spec(off, prev):
        if prev:
            return pl.BlockSpec((BLK, LANES), lambda b: (jnp.maximum(b - 1, 0), k_blk + off))
        return pl.BlockSpec((BLK, LANES), lambda b: (b, k_blk + off))

    in_specs = [pl.BlockSpec((BLK, LANES), lambda b, c=q_blk + j: (b, c)) for j in range(n_q)]
    in_specs += [kv_spec(0, True), kv_spec(0, False), kv_spec(1, True), kv_spec(1, False)]
    return pl.pallas_call(
        functools.partial(_band_kernel, n_q=n_q, n_kv=1, head_cfg=head_cfg, has_sink=True, want_lse=False,
                          dilation=1),
        out_shape=jax.ShapeDtypeStruct((s_len, B_Q_COLS), f32),
        grid=(s_len // BLK,),
        in_specs=in_specs + [_const_spec((B_Q_HEADS, BLK, 2 * BLK)), pl.BlockSpec(memory_space=pltpu.SMEM)],
        out_specs=pl.BlockSpec((BLK, B_Q_COLS), lambda b: (b, 0)),
        compiler_params=_params(("parallel",)),
        name="band_b",
    )(*([qkv] * len(in_specs)), bias, sinks)


def _sample_kernel(*refs, groups):
    n_in = sum(n_q + 5 + int(has_sink) for (n_q, _, has_sink, _, _) in groups)
    ins, outs = list(refs[:n_in]), list(refs[n_in:])
    for (n_q, head_cfg, has_sink, want_lse, kv_cols) in groups:
        q_refs = [ins.pop(0) for _ in range(n_q)]
        kn_ref, vn_ref, buf_ref, bo_ref, bn_ref = (ins.pop(0) for _ in range(5))
        sink_ref = ins.pop(0) if has_sink else None
        o_ref = outs.pop(0)
        l_ref = outs.pop(0) if want_lse else None
        nb_ref = outs.pop(0)
        _sample_group(q_refs, kn_ref, vn_ref, buf_ref, bo_ref, bn_ref, sink_ref, o_ref, l_ref, nb_ref,
                      head_cfg, kv_cols)


def _sample_group(q_refs, kn_ref, vn_ref, buf_ref, bo_ref, bn_ref, sink_ref, o_ref, l_ref, nb_ref, head_cfg, kv_cols):
    has_sink = sink_ref is not None
    want_lse = l_ref is not None

    buf_len = buf_ref.shape[2]
    t_new = DEC_SEQ
    k_new = kn_ref[...]
    v_new = vn_ref[...]
    pad = jnp.zeros((LANES - t_new, kv_cols), f32)
    k_pad = jnp.concatenate([k_new, pad], axis=0)
    v_pad = jnp.concatenate([v_new, pad], axis=0)
    buf = buf_ref[0]

    new_t = jnp.concatenate([k_pad, v_pad], axis=1).T
    shifted = pltpu.roll(buf, buf_len - t_new, 1)
    lane = lax.broadcasted_iota(jnp.int32, (2 * kv_cols, LANES), 1)
    tail = jnp.where(lane >= LANES - t_new, pltpu.roll(new_t, LANES - t_new, 1), shifted[:, buf_len - LANES:])
    if buf_len > LANES:
        nb_ref[0, :, 0:buf_len - LANES] = shifted[:, 0:buf_len - LANES]
    nb_ref[0, :, buf_len - LANES:buf_len] = tail

    rows = []
    for (qi, qoff, kvoff) in head_cfg:
        q = q_refs[qi][:, qoff:qoff + HEAD_DIM]
        pieces = []
        if kvoff > 0:
            pieces.append(jnp.zeros((t_new, kvoff), f32))
        pieces.append(q)
        if kv_cols - kvoff - HEAD_DIM > 0:
            pieces.append(jnp.zeros((t_new, kv_cols - kvoff - HEAD_DIM), f32))
        rows.append(jnp.concatenate(pieces, axis=1) if len(pieces) > 1 else q)
    qrows = jnp.concatenate(rows, axis=0).astype(bf16)

    kt_old = buf[0:kv_cols, :].astype(bf16)
    vt_old = buf[kv_cols:2 * kv_cols, :].astype(bf16)
    k_pad = k_pad.astype(bf16)
    v_pad = v_pad.astype(bf16)

    contract_last = (((1,), (1,)), ((), ()))
    s_old = jnp.dot(qrows, kt_old, preferred_element_type=f32) * SCALE + bo_ref[...]
    s_new = lax.dot_general(qrows, k_pad, contract_last, preferred_element_type=f32) * SCALE + bn_ref[...]
    m = jnp.maximum(jnp.max(s_old, axis=-1, keepdims=True), jnp.max(s_new, axis=-1, keepdims=True))
    n_heads = len(head_cfg)
    sink_col = None
    if has_sink:
        sink_col = jnp.concatenate(
            [jnp.full((t_new, 1), sink_ref[h], f32) for h in range(n_heads)], axis=0)
        m = jnp.maximum(m, sink_col)
    e_old = jnp.exp(s_old - m)
    e_new = jnp.exp(s_new - m)
    den = jnp.sum(e_old, axis=-1, keepdims=True) + jnp.sum(e_new, axis=-1, keepdims=True)
    if has_sink:
        den = den + jnp.exp(sink_col - m)
    acc = (lax.dot_general((e_old / den).astype(bf16), vt_old, contract_last, preferred_element_type=f32)
           + jnp.dot((e_new / den).astype(bf16), v_pad, preferred_element_type=f32))
    lse = m + jnp.log(den)
    for h, (qi, qoff, kvoff) in enumerate(head_cfg):
        o_ref[:, h * HEAD_DIM:(h + 1) * HEAD_DIM] = acc[h * t_new:(h + 1) * t_new, kvoff:kvoff + HEAD_DIM]
        if want_lse:
            l_ref[:, h * HEAD_DIM:(h + 1) * HEAD_DIM] = jnp.broadcast_to(
                lse[h * t_new:(h + 1) * t_new, :], (t_new, HEAD_DIM))


def _position_minor(buf):
    bd, buf_len = buf.shape[:2]
    return jnp.transpose(buf, (0, 2, 3, 4, 1)).reshape(bd, -1, buf_len)


def _position_major(buf_t, shape):
    bd, buf_len, two, groups, hd = shape
    return jnp.transpose(buf_t.reshape(bd, two, groups, hd, buf_len), (0, 4, 1, 2, 3))


def _sample_attend(qkv, group_args):
    bd = group_args[0][0].shape[0]
    assert qkv.shape[0] == bd * DEC_SEQ

    def col_spec(width, idx):
        return pl.BlockSpec((DEC_SEQ, width), lambda b: (b, idx))

    in_specs, args, out_shape, out_specs, groups = [], [], [], [], []
    for (buf, q_blocks, k_block, v_block, kv_cols, head_cfg, bias_old, bias_new, sinks, want_lse) in group_args:
        _, width2, buf_len = buf.shape
        assert width2 == 2 * kv_cols and buf.shape[0] == bd and buf_len % LANES == 0
        rows = len(head_cfg) * DEC_SEQ
        in_specs += [col_spec(w, i) for (w, i) in q_blocks]
        in_specs += [col_spec(*k_block), col_spec(*v_block),
                     pl.BlockSpec((1, width2, buf_len), lambda b: (b, 0, 0)),
                     _const_spec((rows, buf_len)), _const_spec((rows, LANES))]
        args += [qkv] * (len(q_blocks) + 2) + [buf, bias_old, bias_new]
        if sinks is not None:
            in_specs.append(pl.BlockSpec(memory_space=pltpu.SMEM))
            args.append(sinks)
        out_w = len(head_cfg) * HEAD_DIM
        for _ in range(2 if want_lse else 1):
            out_shape.append(jax.ShapeDtypeStruct((bd * DEC_SEQ, out_w), f32))
            out_specs.append(pl.BlockSpec((DEC_SEQ, out_w), lambda b: (b, 0)))
        out_shape.append(jax.ShapeDtypeStruct(buf.shape, f32))
        out_specs.append(pl.BlockSpec((1, width2, buf_len), lambda b: (b, 0, 0)))
        groups.append((len(q_blocks), head_cfg, sinks is not None, want_lse, kv_cols))
    return pl.pallas_call(
        functools.partial(_sample_kernel, groups=tuple(groups)),
        out_shape=tuple(out_shape),
        grid=(bd,),
        in_specs=in_specs,
        out_specs=tuple(out_specs),
        compiler_params=_params(("parallel",)),
        name="sample",
    )(*args)


def _finish_kernel(x_ref, oa0_ref, oa1_ref, oa2_ref, la0_ref, la1_ref, la2_ref, ob_ref, gate_ref,
                   wa_ref, wb_ref, wo_ref, g2_ref, h_ref, hnt_ref):
    l0, l1, l2 = la0_ref[...], la1_ref[...], la2_ref[...]
    m = jnp.maximum(jnp.maximum(l0, l1), l2)
    e0, e1, e2 = jnp.exp(l0 - m), jnp.exp(l1 - m), jnp.exp(l2 - m)
    den = e0 + e1 + e2
    oa = (e0 / den) * oa0_ref[...] + (e1 / den) * oa1_ref[...] + (e2 / den) * oa2_ref[...]
    ya = jnp.dot(oa.astype(bf16), wa_ref[...], preferred_element_type=f32)
    yb = jnp.dot(ob_ref[...].astype(bf16), wb_ref[...], preferred_element_type=f32)
    merged = gate_ref[:, :D_MODEL] * ya + gate_ref[:, D_MODEL:] * yb
    h = x_ref[...] + jnp.dot(merged.astype(bf16), wo_ref[...], preferred_element_type=f32)
    h_ref[...] = h
    ms = jnp.mean(h * h, axis=-1, keepdims=True)
    hn = (h * lax.rsqrt(ms + EPS)) * g2_ref[...]
    hnt_ref[...] = hn.T.astype(bf16)


def _finish(x, outs_a, lses_a, out_b, gates, wa_bf, wb_bf, wo_bf, norm2):
    n = x.shape[0]
    tm = PROJ_TM
    assert n % tm == 0
    aw = A_HEADS * HEAD_DIM

    def row_spec(w):
        return pl.BlockSpec((tm, w), lambda i: (i, 0))

    return pl.pallas_call(
        _finish_kernel,
        out_shape=(jax.ShapeDtypeStruct((n, D_MODEL), f32), jax.ShapeDtypeStruct((D_MODEL, n), bf16)),
        grid=(n // tm,),
        in_specs=[row_spec(D_MODEL)] + [row_spec(aw)] * 6 + [row_spec(B_Q_COLS), row_spec(GATE_COLS),
                  _const_spec((aw, D_MODEL)), _const_spec((B_Q_COLS, D_MODEL)), _const_spec((D_MODEL, D_MODEL)),
                  _const_spec((1, D_MODEL))],
        out_specs=(row_spec(D_MODEL), pl.BlockSpec((D_MODEL, tm), lambda i: (0, i))),
        compiler_params=_params(("parallel",)),
        name="finish",
    )(x, *outs_a, *lses_a, out_b, gates, wa_bf, wb_bf, wo_bf, norm2.reshape(1, D_MODEL))


def _batcher_pairs(n):
    pairs = []
    p = 1
    while p < n:
        k = p
        while k >= 1:
            for j in range(k % p, n - k, 2 * k):
                for i in range(min(k, n - j - k)):
                    if (i + j) // (2 * p) == (i + j + k) // (2 * p):
                        pairs.append((i + j, i + j + k))
            k //= 2
        p *= 2
    return pairs


_SORT16 = _batcher_pairs(PEER_TOPK)
_BITONIC16 = [(i, i + dist) for dist in (8, 4, 2, 1) for i in range(PEER_TOPK) if not i & dist]


def _exchange(xs, i, j):
    a, b = xs[i], xs[j]
    if b is None:
        return
    if a is None:
        xs[i], xs[j] = b, None
        return
    xs[i], xs[j] = jnp.maximum(a, b), jnp.minimum(a, b)


def _sort16_desc(xs):
    xs = list(xs)
    for i, j in _SORT16:
        _exchange(xs, i, j)
    return xs


def _merge_top16(xs, ys):
    zs = []
    for j in range(PEER_TOPK):
        a, b = xs[j], ys[PEER_TOPK - 1 - j]
        zs.append(b if a is None else a if b is None else jnp.maximum(a, b))
    for i, j in _BITONIC16:
        _exchange(zs, i, j)
    return zs


_CAND_PAIRS = [(a, b) for a in range(PEER_TOPK) for b in range(PEER_TOPK) if (a + 1) * (b + 1) <= PEER_TOPK]


def _count_prefix(values, test):
    assert len(values) == PEER_TOPK == 16
    bits = []
    for level in range(4):
        step = 8 >> level
        pivots = [values[i + step - 1] for i in range(0, PEER_TOPK, 2 * step)]
        for bit in reversed(bits):
            pivots = [jnp.where(bit, pivots[j + 1], pivots[j]) for j in range(0, len(pivots), 2)]
        bits.append(test(pivots[0]))
    count = jnp.where(bits[0], 8.0, 0.0)
    for level in range(1, 4):
        count = count + jnp.where(bits[level], float(8 >> level), 0.0)
    return jnp.where(test(values[PEER_TOPK - 1]), float(PEER_TOPK), count)


def _peer_route(hnt_ref, wq_ref, sk1_ref, sk2_ref, n1_ref, s2_ref, e1_ref, r2_ref, e2_ref, top1_ref, top2_ref):
    hnt = hnt_ref[...]
    for half, (sk_ref, s_ref, top_ref) in enumerate(((sk1_ref, n1_ref, top1_ref), (sk2_ref, s2_ref, top2_ref))):
        for h in range(PEER_HEADS):
            r0 = (2 * h + half) * PEER_HALF
            qh = jnp.dot(wq_ref[r0:r0 + PEER_HALF, :], hnt, preferred_element_type=f32).astype(bf16)
            st = jnp.dot(sk_ref[...], qh, preferred_element_type=f32)
            s_ref[h] = st
            xs = _sort16_desc([st[j * SUBLANES:(j + 1) * SUBLANES, :] for j in range(N_SUBKEYS // SUBLANES)])
            for shift in (4, 2, 1):
                xs = _merge_top16(xs, [pltpu.roll(x, shift, 0) for x in xs])
            for a in range(PEER_TOPK):
                top_ref[a, h:h + 1, :] = xs[a][0:1, :]
    v1 = [top1_ref[a] for a in range(PEER_TOPK)]
    v2 = [top2_ref[a] for a in range(PEER_TOPK)]
    cands = [v1[a] + v2[b] for a, b in _CAND_PAIRS]
    cands += [None] * (-len(cands) % PEER_TOPK)
    groups = [_sort16_desc(cands[i::len(cands) // PEER_TOPK]) for i in range(len(cands) // PEER_TOPK)]
    while len(groups) > 1:
        groups = [_merge_top16(groups[i], groups[i + 1]) for i in range(0, len(groups), 2)]
    sc = groups[0]
    den = jnp.zeros_like(sc[0])
    for k in range(PEER_TOPK):
        den = den + jnp.exp(sc[k] - sc[0])
    inv_den = 1.0 / den
    tau = sc[PEER_TOPK - 1]
    tile = 2 * SUBLANES

    def count_tile(j, carry):
        rows = pl.ds(pl.multiple_of(j * tile, tile), tile)
        for h in range(PEER_HEADS):
            head = slice(h, h + 1)
            s1, s2 = n1_ref[h, rows, :], s2_ref[h, rows, :]
            e1_ref[h, rows, :] = jnp.exp(s1 - top1_ref[0, head, :]) * inv_den[head, :]
            e2_ref[h, rows, :] = jnp.exp(s2 - top2_ref[0, head, :]).astype(bf16)
            best = [top2_ref[b, head, :] for b in range(PEER_TOPK)]
            tau_h = tau[head, :]
            n1_ref[h, rows, :] = _count_prefix(best, lambda x: s1 + x >= tau_h)
            r2_ref[h, rows, :] = _count_prefix(best, lambda x: x > s2).astype(bf16)
        return carry

    lax.fori_loop(0, N_SUBKEYS // tile, count_tile, 0)


def _gelu_exact(x):
    return 0.5 * x * (1.0 + lax.erf(x * (2.0 ** -0.5)))


def _peer_kernel(hnt_ref, h_ref, wq_ref, sk1_ref, sk2_ref, u_ref, vt_ref, y_ref,
                 n1_ref, s2_ref, e1_ref, r2_ref, e2_ref, top1_ref, top2_ref, hu_ref, act_ref, acc_ref):
    eb = pl.program_id(1)
    n_sub = PEER_EB // PEER_SUB
    blocks = PEER_SUB // N_SUBKEYS

    @pl.when(eb == 0)
    def _():
        _peer_route(hnt_ref, wq_ref, sk1_ref, sk2_ref, n1_ref, s2_ref, e1_ref, r2_ref, e2_ref, top1_ref, top2_ref)
        acc_ref[...] = jnp.zeros_like(acc_ref)

    def project_sub(k, slot):
        hu_ref[slot] = jnp.dot(u_ref[k], hnt_ref[...], preferred_element_type=f32)

    def activate_sub(k, slot):
        for i in range(blocks):
            local = k * blocks + i
            base = eb * (PEER_EB // N_SUBKEYS) + (local // SUBLANES) * SUBLANES
            group = pl.ds(pl.multiple_of(base, SUBLANES), SUBLANES)
            to_top = (SUBLANES - local % SUBLANES) % SUBLANES
            rows = slice(i * N_SUBKEYS, (i + 1) * N_SUBKEYS)
            for c in range(PEER_T // LANES):
                tok = slice(c * LANES, (c + 1) * LANES)
                g = jnp.zeros((N_SUBKEYS, LANES), bf16)
                for h in range(PEER_HEADS):
                    n1_row = pltpu.roll(n1_ref[h, group, tok], to_top, 0)[0:1, :].astype(bf16)
                    e1_row = pltpu.roll(e1_ref[h, group, tok], to_top, 0)[0:1, :].astype(bf16)
                    w = e1_row * e2_ref[h, :, tok]
                    g = g + jnp.where(r2_ref[h, :, tok] < n1_row, w, jnp.zeros_like(w))
                act = g.astype(f32) * _gelu_exact(hu_ref[slot, rows, tok])
                act_ref[slot, rows, tok] = act.astype(bf16)

    def combine_sub(k, slot):
        acc_ref[...] += jnp.dot(vt_ref[k], act_ref[slot], preferred_element_type=f32)

    project_sub(0, 0)
    for k in range(n_sub):
        if k + 1 < n_sub:
            project_sub(k + 1, (k + 1) % 2)
        activate_sub(k, k % 2)
        combine_sub(k, k % 2)

    @pl.when(eb == pl.num_programs(1) - 1)
    def _():
        y_ref[...] = h_ref[...] + acc_ref[...].T


def _peer(hnt, h, wq_t, sk1, sk2, u_bf, vt_bf):
    n = h.shape[0]
    t = PEER_T
    assert n % t == 0
    return pl.pallas_call(
        _peer_kernel,
        out_shape=jax.ShapeDtypeStruct((n, D_MODEL), f32),
        grid=(n // t, N_EXPERTS // PEER_EB),
        in_specs=[pl.BlockSpec((D_MODEL, t), lambda j, e: (0, j)),
                  pl.BlockSpec((t, D_MODEL), lambda j, e: (j, 0)),
                  _const_spec((2 * PEER_HEADS * PEER_HALF, D_MODEL)),
                  _const_spec((N_SUBKEYS, PEER_HALF)), _const_spec((N_SUBKEYS, PEER_HALF)),
                  pl.BlockSpec((PEER_EB // PEER_SUB, PEER_SUB, D_MODEL), lambda j, e: (e, 0, 0)),
                  pl.BlockSpec((PEER_EB // PEER_SUB, D_MODEL, PEER_SUB), lambda j, e: (e, 0, 0))],
        out_specs=pl.BlockSpec((t, D_MODEL), lambda j, e: (j, 0)),
        scratch_shapes=[pltpu.VMEM((PEER_HEADS, N_SUBKEYS, t), f32)] * 3 + [
            pltpu.VMEM((PEER_HEADS, N_SUBKEYS, t), bf16),
            pltpu.VMEM((PEER_HEADS, N_SUBKEYS, t), bf16),
            pltpu.VMEM((PEER_TOPK, PEER_HEADS, t), f32),
            pltpu.VMEM((PEER_TOPK, PEER_HEADS, t), f32),
            pltpu.VMEM((2, PEER_SUB, t), f32),
            pltpu.VMEM((2, PEER_SUB, t), bf16),
            pltpu.VMEM((D_MODEL, t), f32)],
        compiler_params=_params(("parallel", "arbitrary")),
        name="peer",
    )(hnt, h, wq_t, sk1, sk2, u_bf, vt_bf)


def kernel(x_prompt, x_sample, cache_a1_kv, cache_a2_kv, cache_a3_kv, cache_b_kv, norm1, w_in, q_norm_a, k_norm_a,
           q_norm_b, k_norm_b, rel_bias_table, sinks_b, w_up_a, w_up_b, w_out, norm2, w_peer_q, sub_keys_1,
           sub_keys_2, expert_u, expert_v):
    batch, seq, _ = x_prompt.shape
    dec_batch, dec_seq, _ = x_sample.shape
    assert batch == 1 and dec_seq == DEC_SEQ

    w_in_bf = w_in.astype(bf16)
    wa_bf, wb_bf, wo_bf = w_up_a.astype(bf16), w_up_b.astype(bf16), w_out.astype(bf16)
    wq_t = w_peer_q.T.astype(bf16)
    sk1, sk2 = sub_keys_1.astype(bf16), sub_keys_2.astype(bf16)
    u_bf = expert_u.astype(bf16).reshape(N_EXPERTS // PEER_SUB, PEER_SUB, D_MODEL)
    vt_bf = jnp.transpose(expert_v.reshape(N_EXPERTS // PEER_SUB, PEER_SUB, D_MODEL), (0, 2, 1)).astype(bf16)
    ones_v = jnp.ones((A_HEADS * HEAD_DIM,), f32)
    gain = jnp.concatenate(
        [jnp.concatenate([jnp.tile(q_norm_a[g], A_HEADS), jnp.tile(k_norm_a[g], A_HEADS), ones_v])
         for g in range(N_A_GROUPS)]
        + [jnp.tile(q_norm_b, B_Q_HEADS), jnp.tile(k_norm_b, B_KV_HEADS), jnp.ones((B_KV_COLS,), f32)]
    ).astype(f32).reshape(1, QKV_COLS)
    lane_head = jnp.arange(MXU_COLS) // HEAD_DIM
    seg = (lane_head[:, None] == lane_head[None, :]).astype(bf16)
    b_heads = slice(N_A_GROUPS * A_HEADS, N_A_GROUPS * A_HEADS + B_Q_HEADS)
    tables_a = [rel_bias_table[:, g * A_HEADS:(g + 1) * A_HEADS] for g in range(N_A_GROUPS)]
    table_b = rel_bias_table[:, b_heads]
    sinks = sinks_b.astype(f32)

    xp = x_prompt.reshape(seq, D_MODEL)
    qkv_p, gates_p = _project(xp, norm1, w_in_bf, gain, seg)
    outs_p, lses_p = [], []
    for g, d in enumerate(A_DILATIONS):
        o, l = _band_attend_a(qkv_p, g, d, _band_bias(tables_a[g], d, A_SPAN))
        outs_p.append(o)
        lses_p.append(l)
    out_b_p = _band_attend_b(qkv_p, _band_bias(table_b, 1, B_WINDOW), sinks)
    h_p, hnt_p = _finish(xp, outs_p, lses_p, out_b_p, gates_p, wa_bf, wb_bf, wo_bf, norm2)
    y_prompt = _peer(hnt_p, h_p, wq_t, sk1, sk2, u_bf, vt_bf).reshape(batch, seq, D_MODEL)

    new_p = []
    for g, d in enumerate(A_DILATIONS):
        lp = min(A_SPAN * d, seq)
        c0 = g * A_GROUP_COLS + A_HEADS * HEAD_DIM
        new_p.append(qkv_p[seq - lp:, c0:c0 + 2 * A_HEADS * HEAD_DIM].reshape(batch, lp, 2, A_HEADS, HEAD_DIM))
    lpb = min(B_WINDOW, seq)
    new_b_prompt = qkv_p[seq - lpb:, A_COLS + B_Q_COLS:].reshape(batch, lpb, 2, B_KV_HEADS, HEAD_DIM)

    xs = x_sample.reshape(dec_batch * dec_seq, D_MODEL)
    qkv_s, gates_s = _project(xs, norm1, w_in_bf, gain, seg)
    aw = A_HEADS * HEAD_DIM
    caches_a = (cache_a1_kv, cache_a2_kv, cache_a3_kv)
    cfg_a = tuple((0, h * HEAD_DIM, h * HEAD_DIM) for h in range(A_HEADS))
    group_args = []
    for g, (d, buf) in enumerate(zip(A_DILATIONS, caches_a)):
        bo, bn = _sample_bias(tables_a[g], d, A_SPAN, buf.shape[1])
        group_args.append((_position_minor(buf), [(aw, 3 * g)], (aw, 3 * g + 1), (aw, 3 * g + 2), aw, cfg_a,
                           bo, bn, None, True))
    bo, bn = _sample_bias(table_b, 1, B_WINDOW, cache_b_kv.shape[1])
    cfg_b = tuple((h // 4, (h % 4) * HEAD_DIM, (h // 4) * HEAD_DIM) for h in range(B_Q_HEADS))
    qw = B_Q_COLS // 2
    k_blk = (A_COLS + B_Q_COLS) // B_KV_COLS
    group_args.append((_position_minor(cache_b_kv), [(qw, A_COLS // qw), (qw, A_COLS // qw + 1)],
                       (B_KV_COLS, k_blk), (B_KV_COLS, k_blk + 1), B_KV_COLS, cfg_b, bo, bn, sinks, False))
    res = _sample_attend(qkv_s, group_args)
    outs_s, lses_s = [res[0], res[3], res[6]], [res[1], res[4], res[7]]
    new_s = [_position_major(res[3 * g + 2], caches_a[g].shape) for g in range(N_A_GROUPS)]
    out_b_s = res[9]
    new_b_sample = _position_major(res[10], cache_b_kv.shape)
    h_s, hnt_s = _finish(xs, outs_s, lses_s, out_b_s, gates_s, wa_bf, wb_bf, wo_bf, norm2)
    y_sample = _peer(hnt_s, h_s, wq_t, sk1, sk2, u_bf, vt_bf).reshape(dec_batch, dec_seq, D_MODEL)

    return (y_prompt, y_sample, new_p[0], new_p[1], new_p[2], new_b_prompt,
            new_s[0], new_s[1], new_s[2], new_b_sample)
```

```python
import functools
import math

import jax
import jax.numpy as jnp
from jax import lax
from jax.experimental import pallas as pl
from jax.experimental.pallas import tpu as pltpu

f32 = jnp.float32
bf16 = jnp.bfloat16

D_MODEL = 1024
HEAD_DIM = 64
A_SPAN = 128
A_DILATIONS = (1, 4, 16)
A_HEADS = 4
N_A_GROUPS = 3
B_WINDOW = 128
B_Q_HEADS = 8
B_KV_HEADS = 2
N_BUCKETS = 32
MAX_DISTANCE = 2048
BLK = 128
DEC_SEQ = 8
A_GROUP_COLS = 3 * A_HEADS * HEAD_DIM
A_COLS = N_A_GROUPS * A_GROUP_COLS
B_Q_COLS = B_Q_HEADS * HEAD_DIM
B_KV_COLS = B_KV_HEADS * HEAD_DIM
QKV_COLS = A_COLS + B_Q_COLS + 2 * B_KV_COLS
GATE_COLS = 2 * D_MODEL
N_SUBKEYS = 128
N_EXPERTS = N_SUBKEYS * N_SUBKEYS
PEER_HEADS = 8
PEER_TOPK = 16
PEER_HALF = 128
EPS = 1e-6
NEG = -1e30
SCALE = HEAD_DIM ** -0.5

LANES = 128
SUBLANES = 8
MXU_COLS = 256
VMEM_LIMIT = 56 * 1024 * 1024

PROJ_TM = 512
PEER_T = 512
PEER_EB = 2048
PEER_SUB = 512
assert (PEER_EB // N_SUBKEYS) % SUBLANES == 0 and PEER_EB % PEER_SUB == 0 and PEER_SUB % N_SUBKEYS == 0


def _const_spec(shape):
    nd = len(shape)
    return pl.BlockSpec(shape, lambda *_: (0,) * nd, pipeline_mode=pl.Buffered(1))


def _params(sem):
    return pltpu.CompilerParams(dimension_semantics=sem, vmem_limit_bytes=VMEM_LIMIT)


_NORM_ALL, _NORM_NONE, _NORM_FIRST_HALF = 0, 1, 2


def _chunk_norm_mode(c):
    col = c * MXU_COLS
    if col < A_COLS:
        return _NORM_NONE if (col % A_GROUP_COLS) == 2 * A_HEADS * HEAD_DIM else _NORM_ALL
    if col < A_COLS + B_Q_COLS:
        return _NORM_ALL
    return _NORM_FIRST_HALF


def _proj_kernel(x_ref, g1_ref, w_ref, gain_ref, seg_ref, qkv_ref, gate_ref):
    x = x_ref[...]
    ms = jnp.mean(x * x, axis=-1, keepdims=True)
    xn = ((x * lax.rsqrt(ms + EPS)) * g1_ref[...]).astype(bf16)
    seg = seg_ref[...]
    for c in range(QKV_COLS // MXU_COLS):
        cols = slice(c * MXU_COLS, (c + 1) * MXU_COLS)
        hb = jnp.dot(xn, w_ref[:, cols], preferred_element_type=f32)
        mode = _chunk_norm_mode(c)
        if mode != _NORM_NONE:
            h2 = hb * hb
            hi = h2.astype(bf16)
            lo = (h2 - hi.astype(f32)).astype(bf16)
            ss = (jnp.dot(hi, seg, preferred_element_type=f32)
                  + jnp.dot(lo, seg, preferred_element_type=f32))
            normed = (hb * lax.rsqrt(ss * (1.0 / HEAD_DIM) + EPS)) * gain_ref[:, cols]
            if mode == _NORM_ALL:
                hb = normed
            else:
                hb = jnp.concatenate([normed[:, :LANES], hb[:, LANES:]], axis=1)
        qkv_ref[:, cols] = hb
    hg = jnp.dot(xn, w_ref[:, QKV_COLS:], preferred_element_type=f32)
    gate_ref[...] = jax.nn.sigmoid(hg)


def _project(x, norm1, w_in_bf, gain, seg):
    n = x.shape[0]
    tm = PROJ_TM
    assert n % tm == 0
    return pl.pallas_call(
        _proj_kernel,
        out_shape=(jax.ShapeDtypeStruct((n, QKV_COLS), f32), jax.ShapeDtypeStruct((n, GATE_COLS), f32)),
        grid=(n // tm,),
        in_specs=[
            pl.BlockSpec((tm, D_MODEL), lambda i: (i, 0)),
            _const_spec((1, D_MODEL)),
            _const_spec((D_MODEL, QKV_COLS + GATE_COLS)),
            _const_spec((1, QKV_COLS)),
            _const_spec((MXU_COLS, MXU_COLS)),
        ],
        out_specs=(pl.BlockSpec((tm, QKV_COLS), lambda i: (i, 0)),
                   pl.BlockSpec((tm, GATE_COLS), lambda i: (i, 0))),
        compiler_params=_params(("parallel",)),
        name="project",
    )(x, norm1.reshape(1, D_MODEL), w_in_bf, gain, seg)


def _t5_bucket(dist):
    exact = N_BUCKETS // 2
    d32 = jnp.maximum(dist, 1).astype(f32)
    large = exact + (jnp.log(d32 / exact) / math.log(MAX_DISTANCE / exact) * (N_BUCKETS - exact)).astype(jnp.int32)
    large = jnp.minimum(large, N_BUCKETS - 1)
    return jnp.where(dist < exact, dist, large)


def _toeplitz(period_vals, rows, width):
    reps = -(-rows * width // (width + 1))
    return jnp.tile(period_vals, (1, reps))[:, :rows * width].reshape(-1, rows, width)


def _band_bias(table_h, d, span):
    k = jnp.arange(2 * BLK + 1)
    steps = BLK - k
    vals = table_h[_t5_bucket(jnp.maximum(steps, 0) * d)].astype(f32).T
    vals = jnp.where(((steps >= 0) & (steps <= span))[None], vals, NEG)
    return _toeplitz(vals, BLK, 2 * BLK)


def _sample_bias(table_h, d, steps, buf_len):
    assert buf_len == steps * d
    width = buf_len + DEC_SEQ
    dist = buf_len - jnp.arange(width + 1)
    ok = (dist >= 0) & (dist % d == 0)
    vals = table_h[_t5_bucket(jnp.maximum(dist, 0))].astype(f32).T
    vals = jnp.where(ok[None], vals, NEG)
    b = _toeplitz(vals, DEC_SEQ, width).reshape(-1, width)
    old = b[:, :buf_len]
    new = jnp.pad(b[:, buf_len:], ((0, 0), (0, LANES - DEC_SEQ)), constant_values=NEG)
    return old, new


def _band_kernel(*refs, n_q, n_kv, head_cfg, has_sink, want_lse, dilation):
    q_refs = refs[:n_q]
    pos = n_q
    kp_refs, kc_refs, vp_refs, vc_refs = (refs[pos + i * n_kv:pos + (i + 1) * n_kv] for i in range(4))
    pos += 4 * n_kv
    bias_ref = refs[pos]
    pos += 1
    sink_ref = None
    if has_sink:
        sink_ref = refs[pos]
        pos += 1
    o_ref = refs[pos]
    pos += 1
    l_ref = None
    if want_lse:
        l_ref = refs[pos]
        pos += 1
    o_stage = refs[pos:pos + n_q] if dilation > 1 else None
    l_stage = refs[pos + n_q:pos + 2 * n_q] if dilation > 1 and want_lse else None

    blk = pl.program_id(0)
    col = lax.broadcasted_iota(jnp.int32, (BLK, 2 * BLK), 1)
    no_prev = jnp.where((col < BLK) & (blk == 0), NEG, 0.0).astype(f32)
    contract_last = (((1,), (1,)), ((), ()))
    heads_per_ref = LANES // HEAD_DIM

    def residue(r, carry):
        rows = pl.ds(r, BLK, stride=dilation) if dilation > 1 else slice(None)
        qs = [ref[rows, :] for ref in q_refs]
        kps, kcs = [ref[rows, :] for ref in kp_refs], [ref[rows, :] for ref in kc_refs]
        vps, vcs = [ref[rows, :] for ref in vp_refs], [ref[rows, :] for ref in vc_refs]
        outs, lses = [], []
        for h, (qi, qoff, ki, koff) in enumerate(head_cfg):
            q = qs[qi][:, qoff:qoff + HEAD_DIM].astype(bf16)
            k = jnp.concatenate([kps[ki][:, koff:koff + HEAD_DIM], kcs[ki][:, koff:koff + HEAD_DIM]],
                                axis=0).astype(bf16)
            v = jnp.concatenate([vps[ki][:, koff:koff + HEAD_DIM], vcs[ki][:, koff:koff + HEAD_DIM]],
                                axis=0).astype(bf16)
            s = lax.dot_general(q, k, contract_last, preferred_element_type=f32) * SCALE + bias_ref[h] + no_prev
            m = jnp.max(s, axis=-1, keepdims=True)
            if has_sink:
                sink = sink_ref[h]
                m = jnp.maximum(m, sink)
            e = jnp.exp(s - m)
            den = jnp.sum(e, axis=-1, keepdims=True)
            if has_sink:
                den = den + jnp.exp(sink - m)
            p = (e / den).astype(bf16)
            outs.append(jnp.dot(p, v, preferred_element_type=f32))
            if want_lse:
                lses.append(jnp.broadcast_to(m + jnp.log(den), (BLK, HEAD_DIM)))
        for j in range(n_q):
            o = jnp.concatenate(outs[j * heads_per_ref:(j + 1) * heads_per_ref], axis=1)
            if dilation > 1:
                o_stage[j][rows, :] = o
            else:
                o_ref[:, j * LANES:(j + 1) * LANES] = o
            if want_lse:
                lse = jnp.concatenate(lses[j * heads_per_ref:(j + 1) * heads_per_ref], axis=1)
                if dilation > 1:
                    l_stage[j][rows, :] = lse
                else:
                    l_ref[:, j * LANES:(j + 1) * LANES] = lse
        return carry

    if dilation > 1:
        lax.fori_loop(0, dilation, residue, 0)
        for j in range(n_q):
            o_ref[:, j * LANES:(j + 1) * LANES] = o_stage[j][...]
            if want_lse:
                l_ref[:, j * LANES:(j + 1) * LANES] = l_stage[j][...]
    else:
        residue(0, 0)


def _band_attend_a(qkv, g, d, bias):
    s_len = qkv.shape[0]
    rows = d * BLK
    assert s_len % rows == 0
    width = A_HEADS * HEAD_DIM
    n_ref = width // LANES
    base = g * 3 * n_ref

    def specs(part, prev):
        if prev:
            return [pl.BlockSpec((rows, LANES), lambda b, c=base + part * n_ref + j: (jnp.maximum(b - 1, 0), c))
                    for j in range(n_ref)]
        return [pl.BlockSpec((rows, LANES), lambda b, c=base + part * n_ref + j: (b, c)) for j in range(n_ref)]

    heads_per_ref = LANES // HEAD_DIM
    head_cfg = tuple((h // heads_per_ref, (h % heads_per_ref) * HEAD_DIM) * 2 for h in range(A_HEADS))
    in_specs = specs(0, False) + specs(1, True) + specs(1, False) + specs(2, True) + specs(2, False)
    scratch = [pltpu.VMEM((rows, LANES), f32)] * (2 * n_ref) if d > 1 else []
    return pl.pallas_call(
        functools.partial(_band_kernel, n_q=n_ref, n_kv=n_ref, head_cfg=head_cfg, has_sink=False, want_lse=True,
                          dilation=d),
        out_shape=(jax.ShapeDtypeStruct((s_len, width), f32), jax.ShapeDtypeStruct((s_len, width), f32)),
        grid=(s_len // rows,),
        in_specs=in_specs + [_const_spec((A_HEADS, BLK, 2 * BLK))],
        out_specs=(pl.BlockSpec((rows, width), lambda b: (b, 0)),
                   pl.BlockSpec((rows, width), lambda b: (b, 0))),
        scratch_shapes=scratch,
        compiler_params=_params(("parallel",)),
        name=f"band_a{g}",
    )(*([qkv] * len(in_specs)), bias)


def _band_attend_b(qkv, bias, sinks):
    s_len = qkv.shape[0]
    assert s_len % BLK == 0
    assert B_KV_COLS == LANES
    n_q = B_Q_COLS // LANES
    q_blk = A_COLS // LANES
    k_blk = (A_COLS + B_Q_COLS) // LANES
    heads_per_ref = LANES // HEAD_DIM
    group = B_Q_HEADS // B_KV_HEADS
    head_cfg = tuple((h // heads_per_ref, (h % heads_per_ref) * HEAD_DIM, 0, (h // group) * HEAD_DIM)
                     for h in range(B_Q_HEADS))

    def kv_spec(off, prev):
        if prev:
            return pl.BlockSpec((BLK, LANES), lambda b: (jnp.maximum(b - 1, 0), k_blk + off))
        return pl.BlockSpec((BLK, LANES), lambda b: (b, k_blk + off))

    in_specs = [pl.BlockSpec((BLK, LANES), lambda b, c=q_blk + j: (b, c)) for j in range(n_q)]
    in_specs += [kv_spec(0, True), kv_spec(0, False), kv_spec(1, True), kv_spec(1, False)]
    return pl.pallas_call(
        functools.partial(_band_kernel, n_q=n_q, n_kv=1, head_cfg=head_cfg, has_sink=True, want_lse=False,
                          dilation=1),
        out_shape=jax.ShapeDtypeStruct((s_len, B_Q_COLS), f32),
        grid=(s_len // BLK,),
        in_specs=in_specs + [_const_spec((B_Q_HEADS, BLK, 2 * BLK)), pl.BlockSpec(memory_space=pltpu.SMEM)],
        out_specs=pl.BlockSpec((BLK, B_Q_COLS), lambda b: (b, 0)),
        compiler_params=_params(("parallel",)),
        name="band_b",
    )(*([qkv] * len(in_specs)), bias, sinks)


def _sample_kernel(*refs, groups):
    n_in = sum(n_q + 5 + int(has_sink) for (n_q, _, has_sink, _, _) in groups)
    ins, outs = list(refs[:n_in]), list(refs[n_in:])
    for (n_q, head_cfg, has_sink, want_lse, kv_cols) in groups:
        q_refs = [ins.pop(0) for _ in range(n_q)]
        kn_ref, vn_ref, buf_ref, bo_ref, bn_ref = (ins.pop(0) for _ in range(5))
        sink_ref = ins.pop(0) if has_sink else None
        o_ref = outs.pop(0)
        l_ref = outs.pop(0) if want_lse else None
        nb_ref = outs.pop(0)
        _sample_group(q_refs, kn_ref, vn_ref, buf_ref, bo_ref, bn_ref, sink_ref, o_ref, l_ref, nb_ref,
                      head_cfg, kv_cols)


def _sample_group(q_refs, kn_ref, vn_ref, buf_ref, bo_ref, bn_ref, sink_ref, o_ref, l_ref, nb_ref, head_cfg, kv_cols):
    has_sink = sink_ref is not None
    want_lse = l_ref is not None

    buf_len = buf_ref.shape[2]
    t_new = DEC_SEQ
    k_new = kn_ref[...]
    v_new = vn_ref[...]
    pad = jnp.zeros((LANES - t_new, kv_cols), f32)
    k_pad = jnp.concatenate([k_new, pad], axis=0)
    v_pad = jnp.concatenate([v_new, pad], axis=0)
    buf = buf_ref[0]

    new_t = jnp.concatenate([k_pad, v_pad], axis=1).T
    shifted = pltpu.roll(buf, buf_len - t_new, 1)
    lane = lax.broadcasted_iota(jnp.int32, (2 * kv_cols, LANES), 1)
    tail = jnp.where(lane >= LANES - t_new, pltpu.roll(new_t, LANES - t_new, 1), shifted[:, buf_len - LANES:])
    if buf_len > LANES:
        nb_ref[0, :, 0:buf_len - LANES] = shifted[:, 0:buf_len - LANES]
    nb_ref[0, :, buf_len - LANES:buf_len] = tail

    rows = []
    for (qi, qoff, kvoff) in head_cfg:
        q = q_refs[qi][:, qoff:qoff + HEAD_DIM]
        pieces = []
        if kvoff > 0:
            pieces.append(jnp.zeros((t_new, kvoff), f32))
        pieces.append(q)
        if kv_cols - kvoff - HEAD_DIM > 0:
            pieces.append(jnp.zeros((t_new, kv_cols - kvoff - HEAD_DIM), f32))
        rows.append(jnp.concatenate(pieces, axis=1) if len(pieces) > 1 else q)
    qrows = jnp.concatenate(rows, axis=0).astype(bf16)

    kt_old = buf[0:kv_cols, :].astype(bf16)
    vt_old = buf[kv_cols:2 * kv_cols, :].astype(bf16)
    k_pad = k_pad.astype(bf16)
    v_pad = v_pad.astype(bf16)

    contract_last = (((1,), (1,)), ((), ()))
    s_old = jnp.dot(qrows, kt_old, preferred_element_type=f32) * SCALE + bo_ref[...]
    s_new = lax.dot_general(qrows, k_pad, contract_last, preferred_element_type=f32) * SCALE + bn_ref[...]
    m = jnp.maximum(jnp.max(s_old, axis=-1, keepdims=True), jnp.max(s_new, axis=-1, keepdims=True))
    n_heads = len(head_cfg)
    sink_col = None
    if has_sink:
        sink_col = jnp.concatenate(
            [jnp.full((t_new, 1), sink_ref[h], f32) for h in range(n_heads)], axis=0)
        m = jnp.maximum(m, sink_col)
    e_old = jnp.exp(s_old - m)
    e_new = jnp.exp(s_new - m)
    den = jnp.sum(e_old, axis=-1, keepdims=True) + jnp.sum(e_new, axis=-1, keepdims=True)
    if has_sink:
        den = den + jnp.exp(sink_col - m)
    acc = (lax.dot_general((e_old / den).astype(bf16), vt_old, contract_last, preferred_element_type=f32)
           + jnp.dot((e_new / den).astype(bf16), v_pad, preferred_element_type=f32))
    lse = m + jnp.log(den)
    for h, (qi, qoff, kvoff) in enumerate(head_cfg):
        o_ref[:, h * HEAD_DIM:(h + 1) * HEAD_DIM] = acc[h * t_new:(h + 1) * t_new, kvoff:kvoff + HEAD_DIM]
        if want_lse:
            l_ref[:, h * HEAD_DIM:(h + 1) * HEAD_DIM] = jnp.broadcast_to(
                lse[h * t_new:(h + 1) * t_new, :], (t_new, HEAD_DIM))


def _position_minor(buf):
    bd, buf_len = buf.shape[:2]
    return jnp.transpose(buf, (0, 2, 3, 4, 1)).reshape(bd, -1, buf_len)


def _position_major(buf_t, shape):
    bd, buf_len, two, groups, hd = shape
    return jnp.transpose(buf_t.reshape(bd, two, groups, hd, buf_len), (0, 4, 1, 2, 3))


def _sample_attend(qkv, group_args):
    bd = group_args[0][0].shape[0]
    assert qkv.shape[0] == bd * DEC_SEQ

    def col_spec(width, idx):
        return pl.BlockSpec((DEC_SEQ, width), lambda b: (b, idx))

    in_specs, args, out_shape, out_specs, groups = [], [], [], [], []
    for (buf, q_blocks, k_block, v_block, kv_cols, head_cfg, bias_old, bias_new, sinks, want_lse) in group_args:
        _, width2, buf_len = buf.shape
        assert width2 == 2 * kv_cols and buf.shape[0] == bd and buf_len % LANES == 0
        rows = len(head_cfg) * DEC_SEQ
        in_specs += [col_spec(w, i) for (w, i) in q_blocks]
        in_specs += [col_spec(*k_block), col_spec(*v_block),
                     pl.BlockSpec((1, width2, buf_len), lambda b: (b, 0, 0)),
                     _const_spec((rows, buf_len)), _const_spec((rows, LANES))]
        args += [qkv] * (len(q_blocks) + 2) + [buf, bias_old, bias_new]
        if sinks is not None:
            in_specs.append(pl.BlockSpec(memory_space=pltpu.SMEM))
            args.append(sinks)
        out_w = len(head_cfg) * HEAD_DIM
        for _ in range(2 if want_lse else 1):
            out_shape.append(jax.ShapeDtypeStruct((bd * DEC_SEQ, out_w), f32))
            out_specs.append(pl.BlockSpec((DEC_SEQ, out_w), lambda b: (b, 0)))
        out_shape.append(jax.ShapeDtypeStruct(buf.shape, f32))
        out_specs.append(pl.BlockSpec((1, width2, buf_len), lambda b: (b, 0, 0)))
        groups.append((len(q_blocks), head_cfg, sinks is not None, want_lse, kv_cols))
    return pl.pallas_call(
        functools.partial(_sample_kernel, groups=tuple(groups)),
        out_shape=tuple(out_shape),
        grid=(bd,),
        in_specs=in_specs,
        out_specs=tuple(out_specs),
        compiler_params=_params(("parallel",)),
        name="sample",
    )(*args)


def _finish_kernel(x_ref, oa0_ref, oa1_ref, oa2_ref, la0_ref, la1_ref, la2_ref, ob_ref, gate_ref,
                   wa_ref, wb_ref, wo_ref, g2_ref, h_ref, hnt_ref):
    l0, l1, l2 = la0_ref[...], la1_ref[...], la2_ref[...]
    m = jnp.maximum(jnp.maximum(l0, l1), l2)
    e0, e1, e2 = jnp.exp(l0 - m), jnp.exp(l1 - m), jnp.exp(l2 - m)
    den = e0 + e1 + e2
    oa = (e0 / den) * oa0_ref[...] + (e1 / den) * oa1_ref[...] + (e2 / den) * oa2_ref[...]
    ya = jnp.dot(oa.astype(bf16), wa_ref[...], preferred_element_type=f32)
    yb = jnp.dot(ob_ref[...].astype(bf16), wb_ref[...], preferred_element_type=f32)
    merged = gate_ref[:, :D_MODEL] * ya + gate_ref[:, D_MODEL:] * yb
    h = x_ref[...] + jnp.dot(merged.astype(bf16), wo_ref[...], preferred_element_type=f32)
    h_ref[...] = h
    ms = jnp.mean(h * h, axis=-1, keepdims=True)
    hn = (h * lax.rsqrt(ms + EPS)) * g2_ref[...]
    hnt_ref[...] = hn.T.astype(bf16)


def _finish(x, outs_a, lses_a, out_b, gates, wa_bf, wb_bf, wo_bf, norm2):
    n = x.shape[0]
    tm = PROJ_TM
    assert n % tm == 0
    aw = A_HEADS * HEAD_DIM

    def row_spec(w):
        return pl.BlockSpec((tm, w), lambda i: (i, 0))

    return pl.pallas_call(
        _finish_kernel,
        out_shape=(jax.ShapeDtypeStruct((n, D_MODEL), f32), jax.ShapeDtypeStruct((D_MODEL, n), bf16)),
        grid=(n // tm,),
        in_specs=[row_spec(D_MODEL)] + [row_spec(aw)] * 6 + [row_spec(B_Q_COLS), row_spec(GATE_COLS),
                  _const_spec((aw, D_MODEL)), _const_spec((B_Q_COLS, D_MODEL)), _const_spec((D_MODEL, D_MODEL)),
                  _const_spec((1, D_MODEL))],
        out_specs=(row_spec(D_MODEL), pl.BlockSpec((D_MODEL, tm), lambda i: (0, i))),
        compiler_params=_params(("parallel",)),
        name="finish",
    )(x, *outs_a, *lses_a, out_b, gates, wa_bf, wb_bf, wo_bf, norm2.reshape(1, D_MODEL))


def _batcher_pairs(n):
    pairs = []
    p = 1
    while p < n:
        k = p
        while k >= 1:
            for j in range(k % p, n - k, 2 * k):
                for i in range(min(k, n - j - k)):
                    if (i + j) // (2 * p) == (i + j + k) // (2 * p):
                        pairs.append((i + j, i + j + k))
            k //= 2
        p *= 2
    return pairs


_SORT16 = _batcher_pairs(PEER_TOPK)
_BITONIC16 = [(i, i + dist) for dist in (8, 4, 2, 1) for i in range(PEER_TOPK) if not i & dist]


def _exchange(xs, i, j):
    a, b = xs[i], xs[j]
    if b is None:
        return
    if a is None:
        xs[i], xs[j] = b, None
        return
    xs[i], xs[j] = jnp.maximum(a, b), jnp.minimum(a, b)


def _sort16_desc(xs):
    xs = list(xs)
    for i, j in _SORT16:
        _exchange(xs, i, j)
    return xs


def _merge_top16(xs, ys):
    zs = []
    for j in range(PEER_TOPK):
        a, b = xs[j], ys[PEER_TOPK - 1 - j]
        zs.append(b if a is None else a if b is None else jnp.maximum(a, b))
    for i, j in _BITONIC16:
        _exchange(zs, i, j)
    return zs


_CAND_PAIRS = [(a, b) for a in range(PEER_TOPK) for b in range(PEER_TOPK) if (a + 1) * (b + 1) <= PEER_TOPK]


def _count_prefix(values, test):
    assert len(values) == PEER_TOPK == 16
    bits = []
    for level in range(4):
        step = 8 >> level
        pivots = [values[i + step - 1] for i in range(0, PEER_TOPK, 2 * step)]
        for bit in reversed(bits):
            pivots = [jnp.where(bit, pivots[j + 1], pivots[j]) for j in range(0, len(pivots), 2)]
        bits.append(test(pivots[0]))
    count = jnp.where(bits[0], 8.0, 0.0)
    for level in range(1, 4):
        count = count + jnp.where(bits[level], float(8 >> level), 0.0)
    return jnp.where(test(values[PEER_TOPK - 1]), float(PEER_TOPK), count)


def _peer_route(hnt_ref, wq_ref, sk1_ref, sk2_ref, n1_ref, s2_ref, e1_ref, r2_ref, e2_ref, top1_ref, top2_ref):
    hnt = hnt_ref[...]
    for half, (sk_ref, s_ref, top_ref) in enumerate(((sk1_ref, n1_ref, top1_ref), (sk2_ref, s2_ref, top2_ref))):
        for h in range(PEER_HEADS):
            r0 = (2 * h + half) * PEER_HALF
            qh = jnp.dot(wq_ref[r0:r0 + PEER_HALF, :], hnt, preferred_element_type=f32).astype(bf16)
            st = jnp.dot(sk_ref[...], qh, preferred_element_type=f32)
            s_ref[h] = st
            xs = _sort16_desc([st[j * SUBLANES:(j + 1) * SUBLANES, :] for j in range(N_SUBKEYS // SUBLANES)])
            for shift in (4, 2, 1):
                xs = _merge_top16(xs, [pltpu.roll(x, shift, 0) for x in xs])
            for a in range(PEER_TOPK):
                top_ref[a, h:h + 1, :] = xs[a][0:1, :]
    v1 = [top1_ref[a] for a in range(PEER_TOPK)]
    v2 = [top2_ref[a] for a in range(PEER_TOPK)]
    cands = [v1[a] + v2[b] for a, b in _CAND_PAIRS]
    cands += [None] * (-len(cands) % PEER_TOPK)
    groups = [_sort16_desc(cands[i::len(cands) // PEER_TOPK]) for i in range(len(cands) // PEER_TOPK)]
    while len(groups) > 1:
        groups = [_merge_top16(groups[i], groups[i + 1]) for i in range(0, len(groups), 2)]
    sc = groups[0]
    den = jnp.zeros_like(sc[0])
    for k in range(PEER_TOPK):
        den = den + jnp.exp(sc[k] - sc[0])
    inv_den = 1.0 / den
    tau = sc[PEER_TOPK - 1]
    tile = 2 * SUBLANES

    def count_tile(j, carry):
        rows = pl.ds(pl.multiple_of(j * tile, tile), tile)
        for h in range(PEER_HEADS):
            head = slice(h, h + 1)
            s1, s2 = n1_ref[h, rows, :], s2_ref[h, rows, :]
            e1_ref[h, rows, :] = jnp.exp(s1 - top1_ref[0, head, :]) * (0.5 * inv_den[head, :])
            e2_ref[h, rows, :] = jnp.exp(s2 - top2_ref[0, head, :]).astype(bf16)
            best = [top2_ref[b, head, :] for b in range(PEER_TOPK)]
            tau_h = tau[head, :]
            n1_ref[h, rows, :] = _count_prefix(best, lambda x: s1 + x >= tau_h)
            r2_ref[h, rows, :] = _count_prefix(best, lambda x: x > s2).astype(bf16)
        return carry

    lax.fori_loop(0, N_SUBKEYS // tile, count_tile, 0)


def _gelu_exact_x2(x):
    return x * (1.0 + lax.erf(x * (2.0 ** -0.5)))


def _peer_kernel(hnt_ref, h_ref, wq_ref, sk1_ref, sk2_ref, u_ref, vt_ref, y_ref,
                 n1_ref, s2_ref, e1_ref, r2_ref, e2_ref, top1_ref, top2_ref, hu_ref, act_ref, acc_ref):
    eb = pl.program_id(1)
    n_sub = PEER_EB // PEER_SUB
    blocks = PEER_SUB // N_SUBKEYS

    @pl.when(eb == 0)
    def _():
        _peer_route(hnt_ref, wq_ref, sk1_ref, sk2_ref, n1_ref, s2_ref, e1_ref, r2_ref, e2_ref, top1_ref, top2_ref)
        acc_ref[...] = jnp.zeros_like(acc_ref)

    def project_sub(k, slot):
        hu_ref[slot] = jnp.dot(u_ref[k], hnt_ref[...], preferred_element_type=f32)

    def activate_sub(k, slot):
        for i in range(blocks):
            local = k * blocks + i
            base = eb * (PEER_EB // N_SUBKEYS) + (local // SUBLANES) * SUBLANES
            group = pl.ds(pl.multiple_of(base, SUBLANES), SUBLANES)
            to_top = (SUBLANES - local % SUBLANES) % SUBLANES
            rows = slice(i * N_SUBKEYS, (i + 1) * N_SUBKEYS)
            for c in range(PEER_T // LANES):
                tok = slice(c * LANES, (c + 1) * LANES)
                g = jnp.zeros((N_SUBKEYS, LANES), bf16)
                for h in range(PEER_HEADS):
                    n1_row = pltpu.roll(n1_ref[h, group, tok], to_top, 0)[0:1, :].astype(bf16)
                    e1_row = pltpu.roll(e1_ref[h, group, tok], to_top, 0)[0:1, :].astype(bf16)
                    w = e1_row * e2_ref[h, :, tok]
                    g = g + jnp.where(r2_ref[h, :, tok] < n1_row, w, jnp.zeros_like(w))
                act_ref[slot, rows, tok] = g * _gelu_exact_x2(hu_ref[slot, rows, tok]).astype(bf16)

    def combine_sub(k, slot):
        acc_ref[...] += jnp.dot(vt_ref[k], act_ref[slot], preferred_element_type=f32)

    project_sub(0, 0)
    for k in range(n_sub):
        if k + 1 < n_sub:
            project_sub(k + 1, (k + 1) % 2)
        activate_sub(k, k % 2)
        combine_sub(k, k % 2)

    @pl.when(eb == pl.num_programs(1) - 1)
    def _():
        y_ref[...] = h_ref[...] + acc_ref[...].T


def _peer(hnt, h, wq_t, sk1, sk2, u_bf, vt_bf):
    n = h.shape[0]
    t = PEER_T
    assert n % t == 0
    return pl.pallas_call(
        _peer_kernel,
        out_shape=jax.ShapeDtypeStruct((n, D_MODEL), f32),
        grid=(n // t, N_EXPERTS // PEER_EB),
        in_specs=[pl.BlockSpec((D_MODEL, t), lambda j, e: (0, j)),
                  pl.BlockSpec((t, D_MODEL), lambda j, e: (j, 0)),
                  _const_spec((2 * PEER_HEADS * PEER_HALF, D_MODEL)),
                  _const_spec((N_SUBKEYS, PEER_HALF)), _const_spec((N_SUBKEYS, PEER_HALF)),
                  pl.BlockSpec((PEER_EB // PEER_SUB, PEER_SUB, D_MODEL), lambda j, e: (e, 0, 0)),
                  pl.BlockSpec((PEER_EB // PEER_SUB, D_MODEL, PEER_SUB), lambda j, e: (e, 0, 0))],
        out_specs=pl.BlockSpec((t, D_MODEL), lambda j, e: (j, 0)),
        scratch_shapes=[pltpu.VMEM((PEER_HEADS, N_SUBKEYS, t), f32)] * 3 + [
            pltpu.VMEM((PEER_HEADS, N_SUBKEYS, t), bf16),
            pltpu.VMEM((PEER_HEADS, N_SUBKEYS, t), bf16),
            pltpu.VMEM((PEER_TOPK, PEER_HEADS, t), f32),
            pltpu.VMEM((PEER_TOPK, PEER_HEADS, t), f32),
            pltpu.VMEM((2, PEER_SUB, t), f32),
            pltpu.VMEM((2, PEER_SUB, t), bf16),
            pltpu.VMEM((D_MODEL, t), f32)],
        compiler_params=_params(("parallel", "arbitrary")),
        name="peer",
    )(hnt, h, wq_t, sk1, sk2, u_bf, vt_bf)


def kernel(x_prompt, x_sample, cache_a1_kv, cache_a2_kv, cache_a3_kv, cache_b_kv, norm1, w_in, q_norm_a, k_norm_a,
           q_norm_b, k_norm_b, rel_bias_table, sinks_b, w_up_a, w_up_b, w_out, norm2, w_peer_q, sub_keys_1,
           sub_keys_2, expert_u, expert_v):
    batch, seq, _ = x_prompt.shape
    dec_batch, dec_seq, _ = x_sample.shape
    assert batch == 1 and dec_seq == DEC_SEQ

    w_in_bf = w_in.astype(bf16)
    wa_bf, wb_bf, wo_bf = w_up_a.astype(bf16), w_up_b.astype(bf16), w_out.astype(bf16)
    wq_t = w_peer_q.T.astype(bf16)
    sk1, sk2 = sub_keys_1.astype(bf16), sub_keys_2.astype(bf16)
    u_bf = expert_u.astype(bf16).reshape(N_EXPERTS // PEER_SUB, PEER_SUB, D_MODEL)
    vt_bf = jnp.transpose(expert_v.reshape(N_EXPERTS // PEER_SUB, PEER_SUB, D_MODEL), (0, 2, 1)).astype(bf16)
    ones_v = jnp.ones((A_HEADS * HEAD_DIM,), f32)
    gain = jnp.concatenate(
        [jnp.concatenate([jnp.tile(q_norm_a[g], A_HEADS), jnp.tile(k_norm_a[g], A_HEADS), ones_v])
         for g in range(N_A_GROUPS)]
        + [jnp.tile(q_norm_b, B_Q_HEADS), jnp.tile(k_norm_b, B_KV_HEADS), jnp.ones((B_KV_COLS,), f32)]
    ).astype(f32).reshape(1, QKV_COLS)
    lane_head = jnp.arange(MXU_COLS) // HEAD_DIM
    seg = (lane_head[:, None] == lane_head[None, :]).astype(bf16)
    b_heads = slice(N_A_GROUPS * A_HEADS, N_A_GROUPS * A_HEADS + B_Q_HEADS)
    tables_a = [rel_bias_table[:, g * A_HEADS:(g + 1) * A_HEADS] for g in range(N_A_GROUPS)]
    table_b = rel_bias_table[:, b_heads]
    sinks = sinks_b.astype(f32)

    xp = x_prompt.reshape(seq, D_MODEL)
    qkv_p, gates_p = _project(xp, norm1, w_in_bf, gain, seg)
    outs_p, lses_p = [], []
    for g, d in enumerate(A_DILATIONS):
        o, l = _band_attend_a(qkv_p, g, d, _band_bias(tables_a[g], d, A_SPAN))
        outs_p.append(o)
        lses_p.append(l)
    out_b_p = _band_attend_b(qkv_p, _band_bias(table_b, 1, B_WINDOW), sinks)
    h_p, hnt_p = _finish(xp, outs_p, lses_p, out_b_p, gates_p, wa_bf, wb_bf, wo_bf, norm2)
    y_prompt = _peer(hnt_p, h_p, wq_t, sk1, sk2, u_bf, vt_bf).reshape(batch, seq, D_MODEL)

    new_p = []
    for g, d in enumerate(A_DILATIONS):
        lp = min(A_SPAN * d, seq)
        c0 = g * A_GROUP_COLS + A_HEADS * HEAD_DIM
        new_p.append(qkv_p[seq - lp:, c0:c0 + 2 * A_HEADS * HEAD_DIM].reshape(batch, lp, 2, A_HEADS, HEAD_DIM))
    lpb = min(B_WINDOW, seq)
    new_b_prompt = qkv_p[seq - lpb:, A_COLS + B_Q_COLS:].reshape(batch, lpb, 2, B_KV_HEADS, HEAD_DIM)

    xs = x_sample.reshape(dec_batch * dec_seq, D_MODEL)
    qkv_s, gates_s = _project(xs, norm1, w_in_bf, gain, seg)
    aw = A_HEADS * HEAD_DIM
    caches_a = (cache_a1_kv, cache_a2_kv, cache_a3_kv)
    cfg_a = tuple((0, h * HEAD_DIM, h * HEAD_DIM) for h in range(A_HEADS))
    group_args = []
    for g, (d, buf) in enumerate(zip(A_DILATIONS, caches_a)):
        bo, bn = _sample_bias(tables_a[g], d, A_SPAN, buf.shape[1])
        group_args.append((_position_minor(buf), [(aw, 3 * g)], (aw, 3 * g + 1), (aw, 3 * g + 2), aw, cfg_a,
                           bo, bn, None, True))
    bo, bn = _sample_bias(table_b, 1, B_WINDOW, cache_b_kv.shape[1])
    cfg_b = tuple((h // 4, (h % 4) * HEAD_DIM, (h // 4) * HEAD_DIM) for h in range(B_Q_HEADS))
    qw = B_Q_COLS // 2
    k_blk = (A_COLS + B_Q_COLS) // B_KV_COLS
    group_args.append((_position_minor(cache_b_kv), [(qw, A_COLS // qw), (qw, A_COLS // qw + 1)],
                       (B_KV_COLS, k_blk), (B_KV_COLS, k_blk + 1), B_KV_COLS, cfg_b, bo, bn, sinks, False))
    res = _sample_attend(qkv_s, group_args)
    outs_s, lses_s = [res[0], res[3], res[6]], [res[1], res[4], res[7]]
    new_s = [_position_major(res[3 * g + 2], caches_a[g].shape) for g in range(N_A_GROUPS)]
    out_b_s = res[9]
    new_b_sample = _position_major(res[10], cache_b_kv.shape)
    h_s, hnt_s = _finish(xs, outs_s, lses_s, out_b_s, gates_s, wa_bf, wb_bf, wo_bf, norm2)
    y_sample = _peer(hnt_s, h_s, wq_t, sk1, sk2, u_bf, vt_bf).reshape(dec_batch, dec_seq, D_MODEL)

    return (y_prompt, y_sample, new_p[0], new_p[1], new_p[2], new_b_prompt,
            new_s[0], new_s[1], new_s[2], new_b_sample)
```
